```python
import math
import jax, jax.numpy as jnp
from jax import lax
import numpy as np

D_MODEL = 2048
BATCH = 2
SEQ = 4096
DEPTH = 4
DEC_BATCH = 8
DEC_SEQ = 1
PAST_LEN = 16384
PAGE_SIZE = 128

N_MEM = 256
MEM_HEADS = 4
MEM_HEAD_DIM = 128
MEM_WIDTH = MEM_HEADS * MEM_HEAD_DIM
MIX_WIDTH = D_MODEL - MEM_WIDTH
CHUNK = 128
N_GROUPS = 12
GROUP_DIM = MIX_WIDTH // N_GROUPS
N_HEADS = 12
HEAD_DIM = MIX_WIDTH // N_HEADS
QK_DIM = HEAD_DIM // 2
Q_BLOCK = 128
D_FF = 5632
CONV_W = 3
EPS = 1e-6
NEG_INF = -1e30
N_LAYERS_A = (DEPTH + 1) // 2
N_LAYERS_B = DEPTH // 2

kernel_name = "hybrid_chunkmlp_diffattn_memxattn_convffn_step"


def rms_norm(x, g):
    xf = x.astype(jnp.float32)
    y = xf * lax.rsqrt(jnp.mean(xf * xf, axis=-1, keepdims=True) + EPS)
    return (y * g.astype(jnp.float32)).astype(x.dtype)


def alibi_slopes(n):
    return 2.0 ** (-8.0 * jnp.arange(1, n + 1, dtype=jnp.float32) / n)


def chunk_spatial_gate(u, v, w_s, b_s):
    B, T, _ = v.shape
    L = min(T, CHUNK)
    mask = jnp.tril(jnp.ones((L, L), dtype=bool))
    w = jnp.where(mask, w_s[:, :L, :L], 0.0)
    vc = v.reshape(B, T // L, L, N_GROUPS, GROUP_DIM)
    mixed = jnp.einsum("gts,bcsge->bctge", w, vc) + b_s[:, :L].T[None, None, :, :, None]
    return u * mixed.reshape(B, T, MIX_WIDTH)


def mixer_a(h, w_in, v_gain, w_s, b_s):
    z = h @ w_in
    uv = jax.nn.gelu(z[..., :2 * MIX_WIDTH])
    u = uv[..., :MIX_WIDTH]
    v = rms_norm(uv[..., MIX_WIDTH:], v_gain)
    return chunk_spatial_gate(u, v, w_s, b_s), v, z[..., 2 * MIX_WIDTH:]


def diff_qkv(h, w_in, q_gain, k_gain):
    z = h @ w_in
    B, T, _ = z.shape
    q = rms_norm(z[..., :MIX_WIDTH].reshape(B, T, N_HEADS, 2, QK_DIM), q_gain)
    k = rms_norm(z[..., MIX_WIDTH:2 * MIX_WIDTH].reshape(B, T, N_HEADS, 2, QK_DIM), k_gain)
    v = z[..., 2 * MIX_WIDTH:3 * MIX_WIDTH].reshape(B, T, N_HEADS, HEAD_DIM)
    return q, k, v, z[..., 3 * MIX_WIDTH:]


def diff_lambda(lq1, lk1, lq2, lk2, lam_init):
    f = lambda a: a.astype(jnp.float32)
    return jnp.exp(jnp.sum(f(lq1) * f(lk1))) - jnp.exp(jnp.sum(f(lq2) * f(lk2))) + lam_init


def diff_head_out(o, subln_g, lam_init):
    B, T = o.shape[:2]
    return (rms_norm(o, subln_g) * (1.0 - lam_init)).reshape(B, T, MIX_WIDTH)


def diff_attn_prompt(q, k, v, lam, lam_init, subln_g):
    B, T = q.shape[:2]
    nqb = T // Q_BLOCK
    slopes = alibi_slopes(N_HEADS)
    kpos = jnp.arange(T, dtype=jnp.float32)
    scale = QK_DIM ** -0.5
    qb = q.reshape(B, nqb, Q_BLOCK, N_HEADS, 2, QK_DIM).transpose(1, 0, 2, 3, 4, 5)

    def block(args):
        qblk, bi = args
        qpos = bi.astype(jnp.float32) * Q_BLOCK + jnp.arange(Q_BLOCK, dtype=jnp.float32)
        dist = qpos[:, None] - kpos[None, :]
        bias = jnp.where(dist >= 0, -slopes[:, None, None] * dist, NEG_INF)
        s = jnp.einsum("bqhmd,bkhmd->mbhqk", qblk, k).astype(jnp.float32) * scale + bias
        p = jax.nn.softmax(s, axis=-1)
        w = (p[0] - lam * p[1]).astype(v.dtype)
        return jnp.einsum("bhqk,bkhd->bqhd", w, v)

    o = lax.map(block, (qb, jnp.arange(nqb)))
    o = o.transpose(1, 0, 2, 3, 4).reshape(B, T, N_HEADS, HEAD_DIM)
    return diff_head_out(o, subln_g, lam_init)


def diff_attn_sample(q, k_new, v_new, cache_k, cache_v, lb, page_table, lam, lam_init, subln_g):
    DB, S = q.shape[:2]
    P = page_table.shape[1] * cache_k.shape[2]
    k_past = cache_k[lb, page_table].reshape(DB, P, N_HEADS, 2, QK_DIM)
    v_past = cache_v[lb, page_table].reshape(DB, P, N_HEADS, HEAD_DIM)
    slopes = alibi_slopes(N_HEADS)
    scale = QK_DIM ** -0.5
    qpos = P + jnp.arange(S, dtype=jnp.float32)
    dist_past = qpos[:, None] - jnp.arange(P, dtype=jnp.float32)[None, :]
    dist_new = qpos[:, None] - qpos[None, :]
    bias_past = -slopes[:, None, None] * dist_past
    bias_new = jnp.where(dist_new >= 0, -slopes[:, None, None] * dist_new, NEG_INF)
    s_past = jnp.einsum("bqhmd,bkhmd->mbhqk", q, k_past).astype(jnp.float32) * scale + bias_past
    s_new = jnp.einsum("bqhmd,bkhmd->mbhqk", q, k_new).astype(jnp.float32) * scale + bias_new
    p = jax.nn.softmax(jnp.concatenate([s_past, s_new], axis=-1), axis=-1)
    w = (p[0] - lam * p[1]).astype(v_new.dtype)
    o = (jnp.einsum("bhqk,bkhd->bqhd", w[..., :P], v_past)
         + jnp.einsum("bhqk,bkhd->bqhd", w[..., P:], v_new))
    return diff_head_out(o, subln_g, lam_init)


def mem_kv(mem, g_norm, w_kv, k_gain):
    B = mem.shape[0]
    m = rms_norm(mem, g_norm) @ w_kv
    k = rms_norm(m[..., :MEM_WIDTH].reshape(B, N_MEM, MEM_HEADS, MEM_HEAD_DIM), k_gain)
    v = m[..., MEM_WIDTH:].reshape(B, N_MEM, MEM_HEADS, MEM_HEAD_DIM)
    return k, v


def mem_attend(q_flat, q_gain, k, v):
    B, T, _ = q_flat.shape
    q = rms_norm(q_flat.reshape(B, T, MEM_HEADS, MEM_HEAD_DIM), q_gain)
    s = jnp.einsum("bthd,bmhd->bhtm", q, k).astype(jnp.float32) * (MEM_HEAD_DIM ** -0.5)
    p = jax.nn.softmax(s, axis=-1).astype(v.dtype)
    return jnp.einsum("bhtm,bmhd->bthd", p, v).reshape(B, T, MEM_WIDTH)


def conv_ffn(h, prev, w_up, cw, cb, w_down):
    z = h @ w_up
    a, b = z[..., :D_FF], z[..., D_FF:]
    T = a.shape[1]
    ap = jnp.concatenate([prev.astype(a.dtype), a], axis=1)
    c = cb
    for j in range(CONV_W):
        c = c + ap[:, j:j + T] * cw[j]
    y = (jax.nn.silu(c) * b) @ w_down
    return y, ap[:, -(CONV_W - 1):]


def _normal(key, shape, scale):
    return jax.random.normal(key, shape, jnp.float32) * scale


def setup_inputs(seed: int = 0) -> dict:
    key = jax.random.key(seed)
    ks = jax.random.split(key, 33)
    n_pages = PAST_LEN // PAGE_SIZE
    n_used = DEC_BATCH * n_pages
    n_pool = n_used + max(1, n_used // 4)
    page_table = jax.random.permutation(ks[0], n_pool)[:n_used].reshape(DEC_BATCH, n_pages).astype(jnp.int32)
    gain = lambda k, shape: 1.0 + _normal(k, shape, 0.02)
    in_a = 2 * MIX_WIDTH + MEM_WIDTH
    in_b = 3 * MIX_WIDTH + MEM_WIDTH
    return {
        "x_prompt": _normal(ks[1], (BATCH, SEQ, D_MODEL), 1.0),
        "x_sample": _normal(ks[2], (DEC_BATCH, DEC_SEQ, D_MODEL), 1.0),
        "mem_prompt": _normal(ks[3], (BATCH, N_MEM, D_MODEL), 1.0),
        "cache_mem_k": _normal(ks[4], (DEPTH, DEC_BATCH, N_MEM, MEM_HEADS, MEM_HEAD_DIM), 1.0),
        "cache_mem_v": _normal(ks[5], (DEPTH, DEC_BATCH, N_MEM, MEM_HEADS, MEM_HEAD_DIM), 1.0),
        "cache_k": _normal(ks[6], (N_LAYERS_B, n_pool, PAGE_SIZE, N_HEADS, HEAD_DIM), 1.0),
        "cache_v": _normal(ks[7], (N_LAYERS_B, n_pool, PAGE_SIZE, N_HEADS, HEAD_DIM), 1.0),
        "state_ffn_conv": _normal(ks[8], (DEPTH, DEC_BATCH, CONV_W - 1, D_FF), 1.0),
        "page_table": page_table,
        "norm_mix": gain(ks[9], (DEPTH, D_MODEL)),
        "norm_mem": gain(ks[10], (DEPTH, D_MODEL)),
        "norm_ffn": gain(ks[11], (DEPTH, D_MODEL)),
        "w_in_a": _normal(ks[12], (N_LAYERS_A, D_MODEL, in_a), D_MODEL ** -0.5),
        "v_norm_a": gain(ks[13], (N_LAYERS_A, MIX_WIDTH)),
        "spatial_w_a": _normal(ks[14], (N_LAYERS_A, N_GROUPS, CHUNK, CHUNK), CHUNK ** -0.5),
        "spatial_b_a": 1.0 + _normal(ks[15], (N_LAYERS_A, N_GROUPS, CHUNK), 0.1),
        "w_out_a": _normal(ks[16], (N_LAYERS_A, MIX_WIDTH + MEM_WIDTH, D_MODEL), (MIX_WIDTH + MEM_WIDTH) ** -0.5),
        "w_in_b": _normal(ks[17], (N_LAYERS_B, D_MODEL, in_b), D_MODEL ** -0.5),
        "q_norm_b": gain(ks[18], (N_LAYERS_B, QK_DIM)),
        "k_norm_b": gain(ks[19], (N_LAYERS_B, QK_DIM)),
        "lambda_q1": _normal(ks[20], (N_LAYERS_B, QK_DIM), 0.1),
        "lambda_k1": _normal(ks[21], (N_LAYERS_B, QK_DIM), 0.1),
        "lambda_q2": _normal(ks[22], (N_LAYERS_B, QK_DIM), 0.1),
        "lambda_k2": _normal(ks[23], (N_LAYERS_B, QK_DIM), 0.1),
        "subln_b": gain(ks[24], (N_LAYERS_B, HEAD_DIM)),
        "w_out_b": _normal(ks[25], (N_LAYERS_B, MIX_WIDTH + MEM_WIDTH, D_MODEL), (MIX_WIDTH + MEM_WIDTH) ** -0.5),
        "w_mem_kv": _normal(ks[26], (DEPTH, D_MODEL, 2 * MEM_WIDTH), D_MODEL ** -0.5),
        "mem_q_norm": gain(ks[27], (DEPTH, MEM_HEAD_DIM)),
        "mem_k_norm": gain(ks[28], (DEPTH, MEM_HEAD_DIM)),
        "w_up": _normal(ks[29], (DEPTH, D_MODEL, 2 * D_FF), D_MODEL ** -0.5),
        "conv_w": _normal(ks[30], (DEPTH, CONV_W, D_FF), CONV_W ** -0.5),
        "conv_b": _normal(ks[31], (DEPTH, D_FF), 0.02),
        "w_down": _normal(ks[32], (DEPTH, D_FF, D_MODEL), D_FF ** -0.5),
    }


def reference(x_prompt, x_sample, mem_prompt, cache_mem_k, cache_mem_v, cache_k, cache_v, state_ffn_conv,
              page_table, norm_mix, norm_mem, norm_ffn, w_in_a, v_norm_a, spatial_w_a, spatial_b_a, w_out_a,
              w_in_b, q_norm_b, k_norm_b, lambda_q1, lambda_k1, lambda_q2, lambda_k2, subln_b, w_out_b,
              w_mem_kv, mem_q_norm, mem_k_norm, w_up, conv_w, conv_b, w_down):
    xp, xs = x_prompt, x_sample
    B = xp.shape[0]
    mem_k_p, mem_v_p, k_p, v_p, conv_p = [], [], [], [], []
    k_s, v_s, conv_s, chunk_v_s = [], [], [], []
    for i in range(DEPTH):
        hp = rms_norm(xp, norm_mix[i])
        hs = rms_norm(xs, norm_mix[i])
        mk, mv = mem_kv(mem_prompt, norm_mem[i], w_mem_kv[i], mem_k_norm[i])
        mem_k_p.append(mk)
        mem_v_p.append(mv)
        if i % 2 == 0:
            la = i // 2
            mix_p, _, qm_p = mixer_a(hp, w_in_a[la], v_norm_a[la], spatial_w_a[la], spatial_b_a[la])
            mix_s, vrow_s, qm_s = mixer_a(hs, w_in_a[la], v_norm_a[la], spatial_w_a[la], spatial_b_a[la])
            chunk_v_s.append(vrow_s)
            w_out = w_out_a[la]
        else:
            lb = i // 2
            lam_init = 0.8 - 0.6 * math.exp(-0.3 * i)
            lam = diff_lambda(lambda_q1[lb], lambda_k1[lb], lambda_q2[lb], lambda_k2[lb], lam_init)
            q, k, v, qm_p = diff_qkv(hp, w_in_b[lb], q_norm_b[lb], k_norm_b[lb])
            mix_p = diff_attn_prompt(q, k, v, lam, lam_init, subln_b[lb])
            k_p.append(k.reshape(k.shape[0], k.shape[1], N_HEADS, HEAD_DIM))
            v_p.append(v)
            q2, k2, v2, qm_s = diff_qkv(hs, w_in_b[lb], q_norm_b[lb], k_norm_b[lb])
            mix_s = diff_attn_sample(q2, k2, v2, cache_k, cache_v, lb, page_table, lam, lam_init, subln_b[lb])
            k_s.append(k2.reshape(k2.shape[0], k2.shape[1], N_HEADS, HEAD_DIM))
            v_s.append(v2)
            w_out = w_out_b[lb]
        mo_p = mem_attend(qm_p, mem_q_norm[i], mk, mv)
        mo_s = mem_attend(qm_s, mem_q_norm[i], cache_mem_k[i], cache_mem_v[i])
        xp = xp + jnp.concatenate([mix_p, mo_p], axis=-1) @ w_out
        xs = xs + jnp.concatenate([mix_s, mo_s], axis=-1) @ w_out
        zeros_prev = jnp.zeros((B, CONV_W - 1, D_FF), xp.dtype)
        fp, cp = conv_ffn(rms_norm(xp, norm_ffn[i]), zeros_prev, w_up[i], conv_w[i], conv_b[i], w_down[i])
        fs, cs = conv_ffn(rms_norm(xs, norm_ffn[i]), state_ffn_conv[i], w_up[i], conv_w[i], conv_b[i], w_down[i])
        xp = xp + fp
        xs = xs + fs
        conv_p.append(cp)
        conv_s.append(cs)
    return (xp, xs, jnp.stack(mem_k_p), jnp.stack(mem_v_p), jnp.stack(k_p), jnp.stack(v_p), jnp.stack(conv_p),
            jnp.stack(k_s), jnp.stack(v_s), jnp.stack(conv_s), jnp.stack(chunk_v_s))
```

```python
import functools
import math

import jax
import jax.numpy as jnp
from jax import lax
from jax.experimental import pallas as pl
from jax.experimental.pallas import tpu as pltpu

F32 = jnp.float32
BF16 = jnp.bfloat16

D_MODEL = 2048
MEM_HEADS = 4
MEM_HEAD_DIM = 128
MEM_WIDTH = MEM_HEADS * MEM_HEAD_DIM
MIX_WIDTH = D_MODEL - MEM_WIDTH
CHUNK = 128
N_GROUPS = 12
N_HEADS = 12
HEAD_DIM = 128
QK_DIM = 64
D_FF = 5632
CONV_W = 3
EPS = 1e-6
NEG_INF = -1e30
SAMPLE_ROWS = 16

VMEM_LIMIT = 56 * 1024 * 1024


def _cparams(*sem):
    return pltpu.CompilerParams(dimension_semantics=sem, vmem_limit_bytes=VMEM_LIMIT)


def _rms(x, g):
    return x * lax.rsqrt(jnp.mean(x * x, axis=-1, keepdims=True) + EPS) * g


def _dot(a, b):
    return jnp.dot(a, b, preferred_element_type=F32)


def _dot_nt(a, b):
    return lax.dot_general(a, b, (((1,), (1,)), ((), ())), preferred_element_type=F32)


def _block_diag_ones(n, group):
    r = jnp.arange(n) // group
    return (r[:, None] == r[None, :]).astype(BF16)


def _in_proj_kernel(x_ref, g_ref, w_ref, bd_ref, *refs, regions, n_gain, group):
    gain_refs = refs[:n_gain]
    n_out = sum((r["f32"] is not None) + (r["bf16"] is not None) for r in regions)
    out_refs = refs[n_gain:n_gain + n_out]
    h_ref = refs[n_gain + n_out]
    j = pl.program_id(1)

    @pl.when(j == 0)
    def _():
        h_ref[...] = _rms(x_ref[...], g_ref[...]).astype(BF16)

    z = _dot(h_ref[...], w_ref[...])

    for r in regions:
        @pl.when((j >= r["start"]) & (j < r["start"] + r["ntiles"]))
        def _(r=r):
            if r["kind"] == "gelu":
                y = jax.nn.gelu(z)
            elif r["kind"] == "plain":
                y = z
            else:
                z2 = z * z
                hi = z2.astype(BF16)
                lo = (z2 - hi.astype(F32)).astype(BF16)
                ss = _dot(hi, bd_ref[...]) + _dot(lo, bd_ref[...])
                y = z * lax.rsqrt(ss * (1.0 / group) + EPS) * gain_refs[r["gain"]][...]
            def store(ref, val, heads=r["heads"]):
                if heads:
                    for hh in range(val.shape[1] // HEAD_DIM):
                        ref[hh] = val[:, hh * HEAD_DIM:(hh + 1) * HEAD_DIM]
                else:
                    ref[...] = val

            if r["f32"] is not None:
                store(out_refs[r["f32"]], y)
            if r["bf16"] is not None:
                store(out_refs[r["bf16"]], (y * r["scale"]).astype(BF16))


def _in_proj(x, g, w, regions, gains, group, tm, tn, seq=None):
    m, d = x.shape
    regs, out_shapes, out_specs = [], [], []
    start = 0
    for r in regions:
        nt = r["width"] // tn
        heads = bool(r.get("heads"))
        reg = dict(start=start, ntiles=nt, kind=r["kind"], gain=r.get("gain"), scale=r.get("scale", 1.0),
                   f32=None, bf16=None, heads=heads)
        for key, dt in (("f32", F32), ("bf16", BF16)):
            if r.get("want_" + key):
                reg[key] = len(out_shapes)
                if heads:
                    tps = seq // tm
                    out_shapes.append(jax.ShapeDtypeStruct((m // seq, r["width"] // HEAD_DIM, seq, HEAD_DIM), dt))
                    out_specs.append(pl.BlockSpec(
                        (None, tn // HEAD_DIM, tm, HEAD_DIM),
                        lambda i, j, s=start, n=nt, tps=tps: (i // tps, jnp.clip(j - s, 0, n - 1), i % tps, 0)))
                else:
                    out_shapes.append(jax.ShapeDtypeStruct((m, r["width"]), dt))
                    out_specs.append(pl.BlockSpec(
                        (tm, tn), lambda i, j, s=start, n=nt: (i, jnp.clip(j - s, 0, n - 1))))
        regs.append(reg)
        start += nt
    assert start * tn == w.shape[1]
    gain_tiles = [jnp.tile(gv.reshape(1, -1), (1, tn // gv.size)) for gv in gains]
    in_specs = [
        pl.BlockSpec((tm, d), lambda i, j: (i, 0)),
        pl.BlockSpec((1, d), lambda i, j: (0, 0)),
        pl.BlockSpec((d, tn), lambda i, j: (0, j)),
        pl.BlockSpec((tn, tn), lambda i, j: (0, 0)),
    ] + [pl.BlockSpec((1, tn), lambda i, j: (0, 0)) for _ in gain_tiles]
    return pl.pallas_call(
        functools.partial(_in_proj_kernel, regions=regs, n_gain=len(gain_tiles), group=group),
        grid=(m // tm, start),
        in_specs=in_specs,
        out_specs=out_specs,
        out_shape=out_shapes,
        scratch_shapes=[pltpu.VMEM((tm, d), BF16)],
        compiler_params=_cparams("parallel", "arbitrary"),
        name="in_proj",
    )(x, g, w, _block_diag_ones(tn, group), *gain_tiles)


def _mixer_a_kernel(u_ref, v_ref, vg_ref, wt_ref, bs_ref, o_ref, *, n_chunks):
    vb = _rms(v_ref[...], vg_ref[...]).astype(BF16)
    for g in range(N_GROUPS):
        cols = slice(g * CHUNK, (g + 1) * CHUNK)
        rhs = jnp.concatenate([vb[c * CHUNK:(c + 1) * CHUNK, cols] for c in range(n_chunks)], axis=1)
        mixed = _dot(wt_ref[g], rhs)
        for c in range(n_chunks):
            rows = slice(c * CHUNK, (c + 1) * CHUNK)
            mc = mixed[:, c * CHUNK:(c + 1) * CHUNK] + bs_ref[g]
            o_ref[rows, cols] = (u_ref[rows, cols] * mc).astype(BF16)


def _mixer_a(u, v, vgain, wt, bs, tm):
    m = u.shape[0]
    row = pl.BlockSpec((tm, MIX_WIDTH), lambda i: (i, 0))
    return pl.pallas_call(
        functools.partial(_mixer_a_kernel, n_chunks=tm // CHUNK),
        grid=(m // tm,),
        in_specs=[row, row,
                  pl.BlockSpec((1, MIX_WIDTH), lambda i: (0, 0)),
                  pl.BlockSpec((N_GROUPS, CHUNK, CHUNK), lambda i: (0, 0, 0)),
                  pl.BlockSpec((N_GROUPS, CHUNK, CHUNK), lambda i: (0, 0, 0))],
        out_specs=row,
        out_shape=jax.ShapeDtypeStruct((m, MIX_WIDTH), BF16),
        compiler_params=_cparams("parallel"),
        name="mixer_a",
    )(u, v, vgain, wt, bs)


def _mixer_a_sample_kernel(u_ref, v_ref, vg_ref, w0_ref, b0_ref, o_ref, vn_ref):
    vn = _rms(v_ref[...], vg_ref[...])
    vn_ref[...] = vn
    mixed = vn.astype(BF16).astype(F32) * w0_ref[...] + b0_ref[...]
    o_ref[...] = (u_ref[...] * mixed).astype(BF16)


def _mixer_a_sample(u, v, vgain, w0, b0):
    m = u.shape[0]
    full = pl.BlockSpec((m, MIX_WIDTH), lambda: (0, 0))
    vec = pl.BlockSpec((1, MIX_WIDTH), lambda: (0, 0))
    return pl.pallas_call(
        _mixer_a_sample_kernel,
        in_specs=[full, full, vec, vec, vec],
        out_specs=[full, full],
        out_shape=[jax.ShapeDtypeStruct((m, MIX_WIDTH), BF16), jax.ShapeDtypeStruct((m, MIX_WIDTH), F32)],
        name="mixer_a_sample",
    )(u, v, vgain, w0, b0)


def _mem_attend_tile(q, qg, k, v):
    outs = []
    for h in range(MEM_HEADS):
        cols = slice(h * MEM_HEAD_DIM, (h + 1) * MEM_HEAD_DIM)
        qn = _rms(q[:, cols], qg).astype(BF16)
        s = _dot_nt(qn, k[:, cols].astype(BF16)) * (MEM_HEAD_DIM ** -0.5)
        s = s - jnp.max(s, axis=-1, keepdims=True)
        e = jnp.exp(s)
        p = e / jnp.sum(e, axis=-1, keepdims=True)
        outs.append(_dot(p.astype(BF16), v[:, cols].astype(BF16)))
    return jnp.concatenate(outs, axis=1)


def _mem_attend_kernel(q_ref, qg_ref, k_ref, v_ref, o_ref):
    o_ref[...] = _mem_attend_tile(q_ref[...], qg_ref[...], k_ref[...], v_ref[...]).astype(BF16)


def _mem_attend(q, qg, k, v, rows_per_batch, tm):
    m = q.shape[0]
    n_mem = k.shape[0] // (m // rows_per_batch)
    tiles_per_batch = rows_per_batch // tm
    kv = pl.BlockSpec((n_mem, MEM_WIDTH), lambda i: (i // tiles_per_batch, 0))
    return pl.pallas_call(
        _mem_attend_kernel,
        grid=(m // tm,),
        in_specs=[pl.BlockSpec((tm, MEM_WIDTH), lambda i: (i, 0)),
                  pl.BlockSpec((1, MEM_HEAD_DIM), lambda i: (0, 0)), kv, kv],
        out_specs=pl.BlockSpec((tm, MEM_WIDTH), lambda i: (i, 0)),
        out_shape=jax.ShapeDtypeStruct((m, MEM_WIDTH), BF16),
        compiler_params=_cparams("parallel"),
        name="mem_attend",
    )(q, qg, k, v)


def _mem_attend_sample_kernel(q_ref, qg_ref, k_ref, v_ref, o_ref):
    b = pl.program_id(0)
    o = _mem_attend_tile(q_ref[...], qg_ref[...], k_ref[...], v_ref[...]).astype(BF16)

    @pl.when(b == 0)
    def _():
        o_ref[...] = jnp.zeros_like(o_ref)

    row = lax.broadcasted_iota(jnp.int32, o.shape, 0)
    o_ref[...] = jnp.where(row == b, o, o_ref[...])


def _mem_attend_sample(q, qg, k, v, layer, n_seq):
    m = q.shape[0]
    n_mem = k.shape[2]
    kv = pl.BlockSpec((None, None, n_mem, MEM_WIDTH), lambda b: (layer, b, 0, 0))
    full = pl.BlockSpec((m, MEM_WIDTH), lambda b: (0, 0))
    return pl.pallas_call(
        _mem_attend_sample_kernel,
        grid=(n_seq,),
        in_specs=[full, pl.BlockSpec((1, MEM_HEAD_DIM), lambda b: (0, 0)), kv, kv],
        out_specs=full,
        out_shape=jax.ShapeDtypeStruct((m, MEM_WIDTH), BF16),
        compiler_params=_cparams("arbitrary"),
        name="mem_attend_sample",
    )(q, qg, k, v)


def _out_proj_kernel(a_ref, b_ref, wa_ref, wb_ref, x_ref, o_ref):
    o_ref[...] = x_ref[...] + _dot(a_ref[...], wa_ref[...]) + _dot(b_ref[...], wb_ref[...])


def _out_proj(mix, mo, w, x, tm, tn):
    m = x.shape[0]
    assert MIX_WIDTH % MEM_WIDTH == 0
    return pl.pallas_call(
        _out_proj_kernel,
        grid=(m // tm, D_MODEL // tn),
        in_specs=[pl.BlockSpec((tm, MIX_WIDTH), lambda i, j: (i, 0)),
                  pl.BlockSpec((tm, MEM_WIDTH), lambda i, j: (i, 0)),
                  pl.BlockSpec((MIX_WIDTH, tn), lambda i, j: (0, j)),
                  pl.BlockSpec((MEM_WIDTH, tn), lambda i, j: (MIX_WIDTH // MEM_WIDTH, j)),
                  pl.BlockSpec((tm, tn), lambda i, j: (i, j))],
        out_specs=pl.BlockSpec((tm, tn), lambda i, j: (i, j)),
        out_shape=jax.ShapeDtypeStruct((m, D_MODEL), F32),
        compiler_params=_cparams("parallel", "arbitrary"),
        name="out_proj",
    )(mix, mo, w, w, x)


HALO = 16


def _ffn_prompt_kernel(x_ref, halo_ref, g_ref, wa_ref, wb_ref, cw_ref, cb_ref, wd_ref, o_ref, conv_ref,
                       h_ref, a_ref, *, tm, tiles_per_seq):
    i = pl.program_id(0)
    f = pl.program_id(1)

    @pl.when(f == 0)
    def _():
        h_ref[HALO:, :] = _rms(x_ref[...], g_ref[...]).astype(BF16)
        hh = _rms(halo_ref[...], g_ref[...])
        h_ref[:HALO, :] = jnp.where(i % tiles_per_seq == 0, 0.0, hh).astype(BF16)

    a_ref[...] = _dot(h_ref[...], wa_ref[...])
    b = _dot(h_ref[HALO:, :], wb_ref[...])
    c = cb_ref[...]
    for tap in range(CONV_W):
        off = HALO - (CONV_W - 1) + tap
        c = c + a_ref[off:off + tm, :] * cw_ref[tap:tap + 1, :]
    act = (jax.nn.silu(c) * b).astype(BF16)
    y = _dot(act, wd_ref[...])

    @pl.when(f == 0)
    def _():
        o_ref[...] = x_ref[...] + y

    @pl.when(f > 0)
    def _():
        o_ref[...] += y

    conv_ref[...] = a_ref[tm + HALO - 8:tm + HALO, :]


def _ffn_prompt(x, g, w_up, cw, cb, w_down, seq, tm, tf):
    m = x.shape[0]
    nf = D_FF // tf
    tiles_per_seq = seq // tm
    n_seq = m // seq
    halo_blocks = tm // HALO
    out, conv = pl.pallas_call(
        functools.partial(_ffn_prompt_kernel, tm=tm, tiles_per_seq=tiles_per_seq),
        grid=(m // tm, nf),
        in_specs=[pl.BlockSpec((tm, D_MODEL), lambda i, f: (i, 0)),
                  pl.BlockSpec((HALO, D_MODEL), lambda i, f: (jnp.maximum(i * halo_blocks - 1, 0), 0)),
                  pl.BlockSpec((1, D_MODEL), lambda i, f: (0, 0)),
                  pl.BlockSpec((D_MODEL, tf), lambda i, f: (0, f)),
                  pl.BlockSpec((D_MODEL, tf), lambda i, f: (0, nf + f)),
                  pl.BlockSpec((CONV_W, tf), lambda i, f: (0, f)),
                  pl.BlockSpec((1, tf), lambda i, f: (0, f)),
                  pl.BlockSpec((tf, D_MODEL), lambda i, f: (f, 0))],
        out_specs=[pl.BlockSpec((tm, D_MODEL), lambda i, f: (i, 0)),
                   pl.BlockSpec((None, 8, tf), lambda i, f: (i, 0, f))],
        out_shape=[jax.ShapeDtypeStruct((m, D_MODEL), F32),
                   jax.ShapeDtypeStruct((m // tm, 8, D_FF), F32)],
        scratch_shapes=[pltpu.VMEM((tm + HALO, D_MODEL), BF16), pltpu.VMEM((tm + HALO, tf), F32)],
        compiler_params=_cparams("parallel", "arbitrary"),
        name="ffn_prompt",
    )(x, x, g, w_up, w_up, cw, cb, w_down)
    return out, conv[tiles_per_seq - 1::tiles_per_seq, 8 - (CONV_W - 1):, :]


def _ffn_sample_kernel(x_ref, g_ref, wa_ref, wb_ref, cw_ref, cb_ref, p0_ref, p1_ref, wd_ref, o_ref, a_out_ref,
                       h_ref):
    f = pl.program_id(0)

    @pl.when(f == 0)
    def _():
        h_ref[...] = _rms(x_ref[...], g_ref[...]).astype(BF16)

    a = _dot(h_ref[...], wa_ref[...])
    b = _dot(h_ref[...], wb_ref[...])
    a_out_ref[...] = a
    c = cb_ref[...] + p0_ref[...] * cw_ref[0:1, :] + p1_ref[...] * cw_ref[1:2, :] + a * cw_ref[2:3, :]
    y = _dot((jax.nn.silu(c) * b).astype(BF16), wd_ref[...])

    @pl.when(f == 0)
    def _():
        o_ref[...] = x_ref[...] + y

    @pl.when(f > 0)
    def _():
        o_ref[...] += y


def _ffn_sample(x, g, w_up, cw, cb, p0, p1, w_down, tf):
    m = x.shape[0]
    nf = D_FF // tf
    full = pl.BlockSpec((m, D_MODEL), lambda f: (0, 0))
    col = pl.BlockSpec((m, tf), lambda f: (0, f))
    return pl.pallas_call(
        _ffn_sample_kernel,
        grid=(nf,),
        in_specs=[full,
                  pl.BlockSpec((1, D_MODEL), lambda f: (0, 0)),
                  pl.BlockSpec((D_MODEL, tf), lambda f: (0, f)),
                  pl.BlockSpec((D_MODEL, tf), lambda f: (0, nf + f)),
                  pl.BlockSpec((CONV_W, tf), lambda f: (0, f)),
                  pl.BlockSpec((1, tf), lambda f: (0, f)),
                  col, col,
                  pl.BlockSpec((tf, D_MODEL), lambda f: (f, 0))],
        out_specs=[full, col],
        out_shape=[jax.ShapeDtypeStruct((m, D_MODEL), F32), jax.ShapeDtypeStruct((m, D_FF), F32)],
        scratch_shapes=[pltpu.VMEM((m, D_MODEL), BF16)],
        compiler_params=_cparams("arbitrary"),
        name="ffn_sample",
    )(x, g, w_up, w_up, cw, cb, p0, p1, w_down)


def _lambda_from(lam_ref, lam_init):
    v = lam_ref[...]
    a = jnp.sum(v[0:1] * v[1:2], axis=-1, keepdims=True)
    b = jnp.sum(v[2:3] * v[3:4], axis=-1, keepdims=True)
    return jnp.exp(a) - jnp.exp(b) + lam_init


def _diff_finish(acc, l, lam, sg, lam_init, half):
    o = acc[:half] / l[:half] - lam * (acc[half:] / l[half:])
    return _rms(o, sg) * (1.0 - lam_init)


def _diff_attn_prompt_kernel(qi_ref, ki_ref, slope_ref, q_ref, k_ref, v_ref, lam_ref, sg_ref, o_ref,
                             qs_ref, rel_ref, m_ref, l_ref, acc_ref, *, t, lam_init):
    h = pl.program_id(1)
    step = pl.program_id(2)
    qi = qi_ref[step]
    ki = ki_ref[step]
    slope = slope_ref[h]

    @pl.when(step == 0)
    def _():
        row = lax.broadcasted_iota(jnp.int32, (t, t), 0)
        col = lax.broadcasted_iota(jnp.int32, (t, t), 1)
        rel_ref[...] = (col - row).astype(F32) * slope

    @pl.when(ki == 0)
    def _():
        q = q_ref[...]
        lane = lax.broadcasted_iota(jnp.int32, q.shape, 1)
        zero = jnp.zeros_like(q)
        qs_ref[:t, :] = jnp.where(lane < QK_DIM, q, zero)
        qs_ref[t:, :] = jnp.where(lane >= QK_DIM, q, zero)
        m_ref[...] = jnp.full_like(m_ref, NEG_INF)
        l_ref[...] = jnp.zeros_like(l_ref)
        acc_ref[...] = jnp.zeros_like(acc_ref)

    def update(masked):
        s = _dot_nt(qs_ref[...], k_ref[...])
        rel = rel_ref[...]
        if masked:
            row = lax.broadcasted_iota(jnp.int32, (t, t), 0)
            col = lax.broadcasted_iota(jnp.int32, (t, t), 1)
            keep = col <= row
            s = jnp.concatenate([jnp.where(keep, s[:t] + rel, NEG_INF),
                                 jnp.where(keep, s[t:] + rel, NEG_INF)], axis=0)
        else:
            s = jnp.concatenate([s[:t] + rel, s[t:] + rel], axis=0)
        off = slope * ((ki - qi) * t).astype(F32)
        m_old = m_ref[...]
        m_new = jnp.maximum(m_old, jnp.max(s, axis=-1, keepdims=True) + off)
        p = jnp.exp(s - (m_new - off))
        alpha = jnp.exp(m_old - m_new)
        l_ref[...] = alpha * l_ref[...] + jnp.sum(p, axis=-1, keepdims=True)
        acc_ref[...] = alpha * acc_ref[...] + _dot(p.astype(BF16), v_ref[...])
        m_ref[...] = m_new

    @pl.when(ki < qi)
    def _():
        update(False)

    @pl.when(ki == qi)
    def _():
        update(True)
        lam = _lambda_from(lam_ref, lam_init)
        o_ref[...] = _diff_finish(acc_ref[...], l_ref[...], lam, sg_ref[...], lam_init, t).astype(BF16)


def _diff_attn_prompt(q, k, v, slopes, lam_vecs, subln, lam_init, n_seq, seq, t):
    m = n_seq * seq
    nt = seq // t
    pairs = [(a, b) for a in range(nt) for b in range(a + 1)]
    qi = jnp.asarray([p[0] for p in pairs], jnp.int32)
    ki = jnp.asarray([p[1] for p in pairs], jnp.int32)
    qspec = pl.BlockSpec((None, None, t, HEAD_DIM), lambda b, h, s, qi, ki: (b, h, qi[s], 0))
    kspec = pl.BlockSpec((None, None, t, HEAD_DIM), lambda b, h, s, qi, ki: (b, h, ki[s], 0))
    grid_spec = pltpu.PrefetchScalarGridSpec(
        num_scalar_prefetch=2,
        grid=(n_seq, N_HEADS, len(pairs)),
        in_specs=[pl.BlockSpec(memory_space=pltpu.SMEM), qspec, kspec, kspec,
                  pl.BlockSpec((4, QK_DIM), lambda b, h, s, qi, ki: (0, 0)),
                  pl.BlockSpec((1, HEAD_DIM), lambda b, h, s, qi, ki: (0, 0))],
        out_specs=pl.BlockSpec((t, HEAD_DIM), lambda b, h, s, qi, ki: (b * nt + qi[s], h)),
        scratch_shapes=[pltpu.VMEM((2 * t, HEAD_DIM), BF16), pltpu.VMEM((t, t), F32),
                        pltpu.VMEM((2 * t, 1), F32), pltpu.VMEM((2 * t, 1), F32),
                        pltpu.VMEM((2 * t, HEAD_DIM), F32)])
    return pl.pallas_call(
        functools.partial(_diff_attn_prompt_kernel, t=t, lam_init=lam_init),
        grid_spec=grid_spec,
        out_shape=jax.ShapeDtypeStruct((m, MIX_WIDTH), BF16),
        compiler_params=_cparams("parallel", "parallel", "arbitrary"),
        name="diff_attn_prompt",
    )(qi, ki, slopes, q, k, v, lam_vecs, subln)


MAP_ROWS = 16


def _diff_attn_sample_kernel(pt_ref, qm_ref, k_ref, v_ref, bias_ref, slope_ref, kn_ref, vn_ref, lam_ref, sg_ref,
                             o_ref, m_ref, l_ref, acc_ref, *, page, past, lam_init):
    p = pl.program_id(1)
    n_pages = pl.num_programs(1)

    @pl.when(p == 0)
    def _():
        m_ref[...] = jnp.full_like(m_ref, NEG_INF)
        l_ref[...] = jnp.zeros_like(l_ref)
        acc_ref[...] = jnp.zeros_like(acc_ref)

    qm = qm_ref[...]
    s = jnp.einsum("hmd,hkd->hmk", qm, k_ref[...].astype(BF16), preferred_element_type=F32)
    s = s + bias_ref[...]
    off = slope_ref[...] * (p * page - past).astype(F32)
    m_old = m_ref[...]
    m_new = jnp.maximum(m_old, jnp.max(s, axis=-1, keepdims=True) + off)
    e = jnp.exp(s - (m_new - off))
    alpha = jnp.exp(m_old - m_new)
    l_ref[...] = alpha * l_ref[...] + jnp.sum(e, axis=-1, keepdims=True)
    acc_ref[...] = alpha * acc_ref[...] + jnp.einsum(
        "hmk,hkd->hmd", e.astype(BF16), v_ref[...].astype(BF16), preferred_element_type=F32)
    m_ref[...] = m_new

    @pl.when(p == n_pages - 1)
    def _():
        kn = kn_ref[...].astype(BF16).astype(F32)
        vn = vn_ref[...].astype(BF16).astype(F32)
        s_new = jnp.sum(qm.astype(F32) * kn, axis=-1, keepdims=True)
        m_old = m_ref[...]
        m_fin = jnp.maximum(m_old, s_new)
        alpha = jnp.exp(m_old - m_fin)
        e_new = jnp.exp(s_new - m_fin)
        l = alpha * l_ref[...] + e_new
        o = (alpha * acc_ref[...] + e_new * vn) / l
        lam = _lambda_from(lam_ref, lam_init)
        d = o[:, 0:1, :] - lam * o[:, 1:2, :]
        o_ref[...] = _rms(d, sg_ref[...]) * (1.0 - lam_init)


def _diff_attn_sample(page_table, qm, cache_k, cache_v, layer, bias, slope_col, kn, vn, lam_vecs, subln, lam_init):
    n_seq, n_pages = page_table.shape
    page = cache_k.shape[3]
    past = n_pages * page
    pt = page_table.reshape(-1)
    per_seq = lambda rows: pl.BlockSpec((None, N_HEADS, rows, HEAD_DIM), lambda b, p, pt: (b, 0, 0, 0))
    kv = pl.BlockSpec((None, None, N_HEADS, page, HEAD_DIM),
                      lambda b, p, pt: (layer, pt[b * n_pages + p], 0, 0, 0))
    const = lambda shape: pl.BlockSpec(shape, lambda b, p, pt: (0,) * len(shape))
    stat = (N_HEADS, MAP_ROWS, 1)
    grid_spec = pltpu.PrefetchScalarGridSpec(
        num_scalar_prefetch=1,
        grid=(n_seq, n_pages),
        in_specs=[per_seq(MAP_ROWS), kv, kv, const((N_HEADS, MAP_ROWS, page)), const(stat),
                  per_seq(1), per_seq(1), const((4, QK_DIM)), const((1, HEAD_DIM))],
        out_specs=per_seq(1),
        scratch_shapes=[pltpu.VMEM(stat, F32), pltpu.VMEM(stat, F32),
                        pltpu.VMEM((N_HEADS, MAP_ROWS, HEAD_DIM), F32)])
    return pl.pallas_call(
        functools.partial(_diff_attn_sample_kernel, page=page, past=past, lam_init=lam_init),
        grid_spec=grid_spec,
        out_shape=jax.ShapeDtypeStruct((n_seq, N_HEADS, 1, HEAD_DIM), F32),
        compiler_params=_cparams("parallel", "arbitrary"),
        name="diff_attn_sample",
    )(pt, qm, cache_k, cache_v, bias, slope_col, kn, vn, lam_vecs, subln)


def kernel(x_prompt, x_sample, mem_prompt, cache_mem_k, cache_mem_v, cache_k, cache_v, state_ffn_conv, page_table, norm_mix, norm_mem, norm_ffn, w_in_a, v_norm_a, spatial_w_a, spatial_b_a, w_out_a, w_in_b, q_norm_b, k_norm_b, lambda_q1, lambda_k1, lambda_q2, lambda_k2, subln_b, w_out_b, w_mem_kv, mem_q_norm, mem_k_norm, w_up, conv_w, conv_b, w_down):
    n_seq_p, seq, _ = x_prompt.shape
    n_seq_s = x_sample.shape[0]
    depth = norm_mix.shape[0]
    n_mem = mem_prompt.shape[1]
    m_p = n_seq_p * seq
    page = cache_k.shape[2]

    xp = x_prompt.reshape(m_p, D_MODEL)
    xs = jnp.pad(x_sample.reshape(n_seq_s, D_MODEL), ((0, SAMPLE_ROWS - n_seq_s), (0, 0)))
    mem = mem_prompt.reshape(n_seq_p * n_mem, D_MODEL)
    cmk = cache_mem_k.reshape(depth, n_seq_s, n_mem, MEM_WIDTH)
    cmv = cache_mem_v.reshape(depth, n_seq_s, n_mem, MEM_WIDTH)
    ck = cache_k.transpose(0, 1, 3, 2, 4)
    cv = cache_v.transpose(0, 1, 3, 2, 4)
    conv_state = jnp.pad(state_ffn_conv, ((0, 0), (0, SAMPLE_ROWS - n_seq_s), (0, 0), (0, 0)))

    slopes = 2.0 ** (-8.0 * jnp.arange(1, N_HEADS + 1, dtype=F32) / N_HEADS)
    slope_col = jnp.broadcast_to(slopes[:, None, None], (N_HEADS, MAP_ROWS, 1))
    sample_bias = jnp.broadcast_to(slopes[:, None, None] * jnp.arange(page, dtype=F32)[None, None, :],
                                   (N_HEADS, MAP_ROWS, page))
    lane = jnp.arange(HEAD_DIM)
    map_mask = jnp.stack([lane < QK_DIM, lane >= QK_DIM]).astype(F32)
    head_view = lambda a: a[:n_seq_s].reshape(n_seq_s, N_HEADS, 1, HEAD_DIM)

    tril = jnp.tril(jnp.ones((CHUNK, CHUNK), bool))
    row2 = lambda v: v.reshape(1, -1)

    mem_k_p, mem_v_p, k_p, v_p, conv_p = [], [], [], [], []
    k_s, v_s, conv_s, chunk_v_s = [], [], [], []

    for i in range(depth):
        w_up_i = w_up[i].astype(BF16)
        w_down_i = w_down[i].astype(BF16)

        mk, mv = _in_proj(
            mem, row2(norm_mem[i]), w_mem_kv[i].astype(BF16),
            [dict(width=MEM_WIDTH, kind="gnorm", gain=0, want_f32=True),
             dict(width=MEM_WIDTH, kind="plain", want_f32=True)],
            [mem_k_norm[i]], MEM_HEAD_DIM, tm=n_seq_p * n_mem, tn=MEM_WIDTH)
        mem_k_p.append(mk)
        mem_v_p.append(mv)

        if i % 2 == 0:
            la = i // 2
            w_in = w_in_a[la].astype(BF16)
            w_out = w_out_a[la].astype(BF16)
            regions = [dict(width=MIX_WIDTH, kind="gelu", want_f32=True),
                       dict(width=MIX_WIDTH, kind="gelu", want_f32=True),
                       dict(width=MEM_WIDTH, kind="plain", want_f32=True)]
            wt = jnp.where(tril, spatial_w_a[la], 0.0).astype(BF16)
            bs = jnp.broadcast_to(spatial_b_a[la][:, :, None], (N_GROUPS, CHUNK, CHUNK))

            u, v, qm_p = _in_proj(xp, row2(norm_mix[i]), w_in, regions, [], QK_DIM, tm=512, tn=512)
            mix_p = _mixer_a(u, v, row2(v_norm_a[la]), wt, bs, tm=512)

            u, v, qm_s = _in_proj(xs, row2(norm_mix[i]), w_in, regions, [], QK_DIM, tm=SAMPLE_ROWS, tn=512)
            w0 = jnp.repeat(wt[:, 0, 0].astype(F32), CHUNK).reshape(1, MIX_WIDTH)
            b0 = jnp.repeat(spatial_b_a[la][:, 0], CHUNK).reshape(1, MIX_WIDTH)
            mix_s, vrow = _mixer_a_sample(u, v, row2(v_norm_a[la]), w0, b0)
            chunk_v_s.append(vrow[:n_seq_s])
        else:
            lb = i // 2
            lam_init = 0.8 - 0.6 * math.exp(-0.3 * i)
            w_in = w_in_b[lb].astype(BF16)
            w_out = w_out_b[lb].astype(BF16)
            lam_vecs = jnp.stack([lambda_q1[lb], lambda_k1[lb], lambda_q2[lb], lambda_k2[lb]])
            subln = row2(subln_b[lb])
            gains = [q_norm_b[lb], k_norm_b[lb]]
            qk_scale = QK_DIM ** -0.5

            regions = [dict(width=MIX_WIDTH, kind="gnorm", gain=0, want_bf16=True, scale=qk_scale, heads=True),
                       dict(width=MIX_WIDTH, kind="gnorm", gain=1, want_f32=True, want_bf16=True, heads=True),
                       dict(width=MIX_WIDTH, kind="plain", want_f32=True, want_bf16=True, heads=True),
                       dict(width=MEM_WIDTH, kind="plain", want_f32=True)]
            q, kf, kb, vf, vb, qm_p = _in_proj(xp, row2(norm_mix[i]), w_in, regions, gains, QK_DIM, tm=512, tn=512,
                                               seq=seq)
            k_p.append(kf)
            v_p.append(vf)
            mix_p = _diff_attn_prompt(q, kb, vb, slopes, lam_vecs, subln, lam_init, n_seq_p, seq, t=512)

            regions = [dict(width=MIX_WIDTH, kind="gnorm", gain=0, want_f32=True, scale=qk_scale),
                       dict(width=MIX_WIDTH, kind="gnorm", gain=1, want_f32=True),
                       dict(width=MIX_WIDTH, kind="plain", want_f32=True),
                       dict(width=MEM_WIDTH, kind="plain", want_f32=True)]
            q, kf, vf, qm_s = _in_proj(xs, row2(norm_mix[i]), w_in, regions, gains, QK_DIM, tm=SAMPLE_ROWS, tn=512)
            k_s.append(kf[:n_seq_s])
            v_s.append(vf[:n_seq_s])
            qm = jnp.pad(head_view(q * qk_scale) * map_mask[None, None],
                         ((0, 0), (0, 0), (0, MAP_ROWS - 2), (0, 0))).astype(BF16)
            o = _diff_attn_sample(page_table, qm, ck, cv, lb, sample_bias, slope_col, head_view(kf), head_view(vf),
                                  lam_vecs, subln, lam_init)
            mix_s = jnp.pad(o.reshape(n_seq_s, MIX_WIDTH), ((0, SAMPLE_ROWS - n_seq_s), (0, 0))).astype(BF16)

        mo_p = _mem_attend(qm_p, row2(mem_q_norm[i]), mk, mv, rows_per_batch=seq, tm=512)
        mo_s = _mem_attend_sample(qm_s, row2(mem_q_norm[i]), cmk, cmv, i, n_seq_s)
        xp = _out_proj(mix_p, mo_p, w_out, xp, tm=1024, tn=1024)
        xs = _out_proj(mix_s, mo_s, w_out, xs, tm=SAMPLE_ROWS, tn=1024)

        xp, cp = _ffn_prompt(xp, row2(norm_ffn[i]), w_up_i, conv_w[i], row2(conv_b[i]), w_down_i, seq, tm=512, tf=512)
        xs, a_s = _ffn_sample(xs, row2(norm_ffn[i]), w_up_i, conv_w[i], row2(conv_b[i]),
                              conv_state[i, :, 0], conv_state[i, :, 1], w_down_i, tf=512)
        conv_p.append(cp)
        conv_s.append(jnp.stack([state_ffn_conv[i, :, 1], a_s[:n_seq_s]], axis=1))

    heads = lambda a, b, t: a.reshape(b, t, N_HEADS, HEAD_DIM)
    return (xp.reshape(n_seq_p, seq, D_MODEL),
            xs[:n_seq_s].reshape(n_seq_s, 1, D_MODEL),
            jnp.stack(mem_k_p).reshape(depth, n_seq_p, n_mem, MEM_HEADS, MEM_HEAD_DIM),
            jnp.stack(mem_v_p).reshape(depth, n_seq_p, n_mem, MEM_HEADS, MEM_HEAD_DIM),
            jnp.stack(k_p).transpose(0, 1, 3, 2, 4),
            jnp.stack(v_p).transpose(0, 1, 3, 2, 4),
            jnp.stack(conv_p),
            jnp.stack([heads(a, n_seq_s, 1) for a in k_s]),
            jnp.stack([heads(a, n_seq_s, 1) for a in v_s]),
            jnp.stack(conv_s),
            jnp.stack(chunk_v_s).reshape(len(chunk_v_s), n_seq_s, 1, MIX_WIDTH))
```

```python
import functools
import math

import jax
import jax.numpy as jnp
from jax import lax
from jax.experimental import pallas as pl
from jax.experimental.pallas import tpu as pltpu

F32 = jnp.float32
BF16 = jnp.bfloat16

D_MODEL = 2048
MEM_HEADS = 4
MEM_HEAD_DIM = 128
MEM_WIDTH = MEM_HEADS * MEM_HEAD_DIM
MIX_WIDTH = D_MODEL - MEM_WIDTH
CHUNK = 128
N_GROUPS = 12
N_HEADS = 12
HEAD_DIM = 128
QK_DIM = 64
D_FF = 5632
CONV_W = 3
EPS = 1e-6
NEG_INF = -1e30
SAMPLE_ROWS = 16
MXU_WIDTH = 256

VMEM_LIMIT = 56 * 1024 * 1024


def _cparams(*sem):
    return pltpu.CompilerParams(dimension_semantics=sem, vmem_limit_bytes=VMEM_LIMIT)


def _rms(x, g):
    return x * lax.rsqrt(jnp.mean(x * x, axis=-1, keepdims=True) + EPS) * g


def _dot(a, b):
    return jnp.dot(a, b, preferred_element_type=F32)


def _dot_nt(a, b):
    return lax.dot_general(a, b, (((1,), (1,)), ((), ())), preferred_element_type=F32)


def _block_diag_ones(n, group):
    r = jnp.arange(n) // group
    return (r[:, None] == r[None, :]).astype(BF16)


def _in_proj_kernel(x_ref, g_ref, w_ref, bd_ref, *refs, regions, n_gain, group):
    gain_refs = refs[:n_gain]
    n_out = sum((r["f32"] is not None) + (r["bf16"] is not None) for r in regions)
    out_refs = refs[n_gain:n_gain + n_out]
    h_ref = refs[n_gain + n_out]
    j = pl.program_id(1)

    @pl.when(j == 0)
    def _():
        h_ref[...] = _rms(x_ref[...], g_ref[...]).astype(BF16)

    def store(ref, val, layout):
        if layout == "flat":
            ref[...] = val.astype(ref.dtype)
            return
        for hh in range(val.shape[1] // HEAD_DIM):
            head = val[:, hh * HEAD_DIM:(hh + 1) * HEAD_DIM]
            ref[hh] = head.T.astype(ref.dtype) if layout == "cols" else head.astype(ref.dtype)

    for r in regions:
        @pl.when((j >= r["start"]) & (j < r["start"] + r["ntiles"]))
        def _(r=r):
            z = _dot(h_ref[...], w_ref[...])
            if r["kind"] == "gelu":
                y = jax.nn.gelu(z)
            elif r["kind"] == "plain":
                y = z
            else:
                z2 = (z * z).astype(BF16)
                bw = bd_ref.shape[0]
                ss = jnp.concatenate([_dot(z2[:, c:c + bw], bd_ref[...]) for c in range(0, z2.shape[1], bw)], axis=1)
                y = z * lax.rsqrt(ss * (1.0 / group) + EPS) * gain_refs[r["gain"]][...]
            if r["f32"] is not None:
                store(out_refs[r["f32"]], y, r["f32_layout"])
            if r["bf16"] is not None:
                store(out_refs[r["bf16"]], y if r["scale"] == 1.0 else y * r["scale"], r["bf16_layout"])


def _in_proj(x, g, w, layer, regions, gains, group, tm, tn, seq=None, x_buffers=2):
    m, d = x.shape
    regs, out_shapes, out_specs = [], [], []
    start = 0
    for r in regions:
        nt = r["width"] // tn
        reg = dict(start=start, ntiles=nt, kind=r["kind"], gain=r.get("gain"), scale=r.get("scale", 1.0),
                   f32=None, bf16=None)
        for key, dt in (("f32", F32), ("bf16", BF16)):
            layout = r.get(key)
            reg[key + "_layout"] = layout
            if layout is None:
                continue
            reg[key] = len(out_shapes)
            tile = lambda i, j, s=start, n=nt: jnp.clip(j - s, 0, n - 1)
            if layout == "flat":
                out_shapes.append(jax.ShapeDtypeStruct((m, r["width"]), dt))
                out_specs.append(pl.BlockSpec((tm, tn), lambda i, j, tile=tile: (i, tile(i, j))))
                continue
            tps = seq // tm
            heads, hpt = r["width"] // HEAD_DIM, tn // HEAD_DIM
            if layout == "rows":
                out_shapes.append(jax.ShapeDtypeStruct((m // seq, heads, seq, HEAD_DIM), dt))
                out_specs.append(pl.BlockSpec(
                    (None, hpt, tm, HEAD_DIM), lambda i, j, tile=tile, tps=tps: (i // tps, tile(i, j), i % tps, 0)))
            else:
                out_shapes.append(jax.ShapeDtypeStruct((m // seq, heads, HEAD_DIM, seq), dt))
                out_specs.append(pl.BlockSpec(
                    (None, hpt, HEAD_DIM, tm), lambda i, j, tile=tile, tps=tps: (i // tps, tile(i, j), 0, i % tps)))
        regs.append(reg)
        start += nt
    assert start * tn == w.shape[2]
    gain_tiles = [jnp.tile(gv.reshape(1, -1), (1, tn // gv.size)) for gv in gains]
    in_specs = [
        pl.BlockSpec((tm, d), lambda i, j: (i, 0), pipeline_mode=pl.Buffered(x_buffers)),
        pl.BlockSpec((1, d), lambda i, j: (0, 0)),
        pl.BlockSpec((None, d, tn), lambda i, j: (layer, 0, j)),
        pl.BlockSpec((MXU_WIDTH, MXU_WIDTH), lambda i, j: (0, 0)),
    ] + [pl.BlockSpec((1, tn), lambda i, j: (0, 0)) for _ in gain_tiles]
    return pl.pallas_call(
        functools.partial(_in_proj_kernel, regions=regs, n_gain=len(gain_tiles), group=group),
        grid=(m // tm, start),
        in_specs=in_specs,
        out_specs=out_specs,
        out_shape=out_shapes,
        scratch_shapes=[pltpu.VMEM((tm, d), BF16)],
        compiler_params=_cparams("parallel", "arbitrary"),
        name="in_proj",
    )(x, g, w, _block_diag_ones(MXU_WIDTH, group), *gain_tiles)


def _mixer_a_kernel(u_ref, v_ref, vg_ref, wt_ref, bs_ref, o_ref, *, n_chunks):
    vb = _rms(v_ref[...], vg_ref[...]).astype(BF16)
    for g in range(N_GROUPS):
        cols = slice(g * CHUNK, (g + 1) * CHUNK)
        rhs = jnp.concatenate([vb[c * CHUNK:(c + 1) * CHUNK, cols] for c in range(n_chunks)], axis=1)
        mixed = _dot(wt_ref[g], rhs)
        for c in range(n_chunks):
            rows = slice(c * CHUNK, (c + 1) * CHUNK)
            mc = mixed[:, c * CHUNK:(c + 1) * CHUNK] + bs_ref[g]
            o_ref[rows, cols] = (u_ref[rows, cols] * mc).astype(BF16)


def _mixer_a(u, v, vgain, wt, bs, tm):
    m = u.shape[0]
    row = pl.BlockSpec((tm, MIX_WIDTH), lambda i: (i, 0))
    return pl.pallas_call(
        functools.partial(_mixer_a_kernel, n_chunks=tm // CHUNK),
        grid=(m // tm,),
        in_specs=[row, row,
                  pl.BlockSpec((1, MIX_WIDTH), lambda i: (0, 0)),
                  pl.BlockSpec((N_GROUPS, CHUNK, CHUNK), lambda i: (0, 0, 0)),
                  pl.BlockSpec((N_GROUPS, CHUNK, CHUNK), lambda i: (0, 0, 0))],
        out_specs=row,
        out_shape=jax.ShapeDtypeStruct((m, MIX_WIDTH), BF16),
        compiler_params=_cparams("parallel"),
        name="mixer_a",
    )(u, v, vgain, wt, bs)


def _mixer_a_sample_kernel(u_ref, v_ref, vg_ref, w0_ref, b0_ref, o_ref, vn_ref):
    vn = _rms(v_ref[...], vg_ref[...])
    vn_ref[...] = vn
    mixed = vn.astype(BF16).astype(F32) * w0_ref[...] + b0_ref[...]
    o_ref[...] = (u_ref[...] * mixed).astype(BF16)


def _mixer_a_sample(u, v, vgain, w0, b0):
    m = u.shape[0]
    full = pl.BlockSpec((m, MIX_WIDTH), lambda: (0, 0))
    vec = pl.BlockSpec((1, MIX_WIDTH), lambda: (0, 0))
    return pl.pallas_call(
        _mixer_a_sample_kernel,
        in_specs=[full, full, vec, vec, vec],
        out_specs=[full, full],
        out_shape=[jax.ShapeDtypeStruct((m, MIX_WIDTH), BF16), jax.ShapeDtypeStruct((m, MIX_WIDTH), F32)],
        name="mixer_a_sample",
    )(u, v, vgain, w0, b0)


def _mem_attend_tile(q, qg, k, v):
    outs = []
    for h in range(MEM_HEADS):
        cols = slice(h * MEM_HEAD_DIM, (h + 1) * MEM_HEAD_DIM)
        qn = _rms(q[:, cols], qg).astype(BF16)
        s = _dot_nt(qn, k[:, cols].astype(BF16)) * (MEM_HEAD_DIM ** -0.5)
        s = s - jnp.max(s, axis=-1, keepdims=True)
        e = jnp.exp(s)
        p = e / jnp.sum(e, axis=-1, keepdims=True)
        outs.append(_dot(p.astype(BF16), v[:, cols].astype(BF16)))
    return jnp.concatenate(outs, axis=1)


def _mem_attend_kernel(q_ref, qg_ref, k_ref, v_ref, o_ref):
    o_ref[...] = _mem_attend_tile(q_ref[...], qg_ref[...], k_ref[...], v_ref[...]).astype(BF16)


def _mem_attend(q, qg, k, v, rows_per_batch, tm):
    m = q.shape[0]
    n_mem = k.shape[0] // (m // rows_per_batch)
    tiles_per_batch = rows_per_batch // tm
    kv = pl.BlockSpec((n_mem, MEM_WIDTH), lambda i: (i // tiles_per_batch, 0))
    return pl.pallas_call(
        _mem_attend_kernel,
        grid=(m // tm,),
        in_specs=[pl.BlockSpec((tm, MEM_WIDTH), lambda i: (i, 0)),
                  pl.BlockSpec((1, MEM_HEAD_DIM), lambda i: (0, 0)), kv, kv],
        out_specs=pl.BlockSpec((tm, MEM_WIDTH), lambda i: (i, 0)),
        out_shape=jax.ShapeDtypeStruct((m, MEM_WIDTH), BF16),
        compiler_params=_cparams("parallel"),
        name="mem_attend",
    )(q, qg, k, v)


def _mem_attend_sample_kernel(q_ref, qg_ref, k_ref, v_ref, o_ref):
    b = pl.program_id(0)
    o = _mem_attend_tile(q_ref[...], qg_ref[...], k_ref[...], v_ref[...]).astype(BF16)

    @pl.when(b == 0)
    def _():
        o_ref[...] = jnp.zeros_like(o_ref)

    row = lax.broadcasted_iota(jnp.int32, o.shape, 0)
    o_ref[...] = jnp.where(row == b, o, o_ref[...])


def _mem_attend_sample(q, qg, k, v, layer, n_seq):
    m = q.shape[0]
    n_mem = k.shape[2]
    kv = pl.BlockSpec((None, None, n_mem, MEM_WIDTH), lambda b: (layer, b, 0, 0))
    full = pl.BlockSpec((m, MEM_WIDTH), lambda b: (0, 0))
    return pl.pallas_call(
        _mem_attend_sample_kernel,
        grid=(n_seq,),
        in_specs=[full, pl.BlockSpec((1, MEM_HEAD_DIM), lambda b: (0, 0)), kv, kv],
        out_specs=full,
        out_shape=jax.ShapeDtypeStruct((m, MEM_WIDTH), BF16),
        compiler_params=_cparams("arbitrary"),
        name="mem_attend_sample",
    )(q, qg, k, v)


def _out_proj_kernel(a_ref, b_ref, wa_ref, wb_ref, x_ref, o_ref):
    o_ref[...] = x_ref[...] + _dot(a_ref[...], wa_ref[...]) + _dot(b_ref[...], wb_ref[...])


def _out_proj(mix, mo, w, layer, x, tm, tn):
    m = x.shape[0]
    assert MIX_WIDTH % MEM_WIDTH == 0
    return pl.pallas_call(
        _out_proj_kernel,
        grid=(m // tm, D_MODEL // tn),
        in_specs=[pl.BlockSpec((tm, MIX_WIDTH), lambda i, j: (i, 0)),
                  pl.BlockSpec((tm, MEM_WIDTH), lambda i, j: (i, 0)),
                  pl.BlockSpec((None, MIX_WIDTH, tn), lambda i, j: (layer, 0, j)),
                  pl.BlockSpec((None, MEM_WIDTH, tn), lambda i, j: (layer, MIX_WIDTH // MEM_WIDTH, j)),
                  pl.BlockSpec((tm, tn), lambda i, j: (i, j))],
        out_specs=pl.BlockSpec((tm, tn), lambda i, j: (i, j)),
        out_shape=jax.ShapeDtypeStruct((m, D_MODEL), F32),
        compiler_params=_cparams("parallel", "arbitrary"),
        name="out_proj",
    )(mix, mo, w, w, x)


HALO = 16


def _ffn_prompt_kernel(x_ref, halo_ref, g_ref, wa_ref, wb_ref, cw_ref, cb_ref, wd_ref, o_ref, conv_ref,
                       h_ref, a_ref, *, tm, tiles_per_seq):
    i = pl.program_id(0)
    f = pl.program_id(1)

    @pl.when(f == 0)
    def _():
        h_ref[HALO:, :] = _rms(x_ref[...], g_ref[...]).astype(BF16)
        hh = _rms(halo_ref[...], g_ref[...])
        h_ref[:HALO, :] = jnp.where(i % tiles_per_seq == 0, 0.0, hh).astype(BF16)
        o_ref[...] = x_ref[...]

    a_ref[...] = _dot(h_ref[...], wa_ref[...])
    b = _dot(h_ref[HALO:, :], wb_ref[...])
    c = cb_ref[...]
    for tap in range(CONV_W):
        off = HALO - (CONV_W - 1) + tap
        c = c + a_ref[off:off + tm, :] * cw_ref[tap:tap + 1, :]
    act = (jax.nn.silu(c) * b).astype(BF16)
    o_ref[...] += _dot(act, wd_ref[...])
    conv_ref[...] = a_ref[tm + HALO - 8:tm + HALO, :]


def _ffn_prompt(x, g, w_up, cw, cb, w_down, layer, seq, tm, tf):
    m = x.shape[0]
    nf = D_FF // tf
    tiles_per_seq = seq // tm
    halo_blocks = tm // HALO
    out, conv = pl.pallas_call(
        functools.partial(_ffn_prompt_kernel, tm=tm, tiles_per_seq=tiles_per_seq),
        grid=(m // tm, nf),
        in_specs=[pl.BlockSpec((tm, D_MODEL), lambda i, f: (i, 0)),
                  pl.BlockSpec((HALO, D_MODEL), lambda i, f: (jnp.maximum(i * halo_blocks - 1, 0), 0)),
                  pl.BlockSpec((1, D_MODEL), lambda i, f: (0, 0)),
                  pl.BlockSpec((None, D_MODEL, tf), lambda i, f: (layer, 0, f)),
                  pl.BlockSpec((None, D_MODEL, tf), lambda i, f: (layer, 0, nf + f)),
                  pl.BlockSpec((CONV_W, tf), lambda i, f: (0, f)),
                  pl.BlockSpec((1, tf), lambda i, f: (0, f)),
                  pl.BlockSpec((None, tf, D_MODEL), lambda i, f: (layer, f, 0))],
        out_specs=[pl.BlockSpec((tm, D_MODEL), lambda i, f: (i, 0)),
                   pl.BlockSpec((None, 8, tf), lambda i, f: (i, 0, f))],
        out_shape=[jax.ShapeDtypeStruct((m, D_MODEL), F32),
                   jax.ShapeDtypeStruct((m // tm, 8, D_FF), F32)],
        scratch_shapes=[pltpu.VMEM((tm + HALO, D_MODEL), BF16), pltpu.VMEM((tm + HALO, tf), F32)],
        compiler_params=_cparams("parallel", "arbitrary"),
        name="ffn_prompt",
    )(x, x, g, w_up, w_up, cw, cb, w_down)
    return out, conv[tiles_per_seq - 1::tiles_per_seq, 8 - (CONV_W - 1):, :]


def _ffn_sample_kernel(x_ref, g_ref, wa_ref, wb_ref, cw_ref, cb_ref, p0_ref, p1_ref, wd_ref, o_ref, a_out_ref,
                       h_ref):
    f = pl.program_id(0)

    @pl.when(f == 0)
    def _():
        h_ref[...] = _rms(x_ref[...], g_ref[...]).astype(BF16)

    a = _dot(h_ref[...], wa_ref[...])
    b = _dot(h_ref[...], wb_ref[...])
    a_out_ref[...] = a
    c = cb_ref[...] + p0_ref[...] * cw_ref[0:1, :] + p1_ref[...] * cw_ref[1:2, :] + a * cw_ref[2:3, :]
    y = _dot((jax.nn.silu(c) * b).astype(BF16), wd_ref[...])

    @pl.when(f == 0)
    def _():
        o_ref[...] = x_ref[...] + y

    @pl.when(f > 0)
    def _():
        o_ref[...] += y


def _ffn_sample(x, g, w_up, cw, cb, p0, p1, w_down, layer, tf):
    m = x.shape[0]
    nf = D_FF // tf
    full = pl.BlockSpec((m, D_MODEL), lambda f: (0, 0))
    col = pl.BlockSpec((m, tf), lambda f: (0, f))
    return pl.pallas_call(
        _ffn_sample_kernel,
        grid=(nf,),
        in_specs=[full,
                  pl.BlockSpec((1, D_MODEL), lambda f: (0, 0)),
                  pl.BlockSpec((None, D_MODEL, tf), lambda f: (layer, 0, f)),
                  pl.BlockSpec((None, D_MODEL, tf), lambda f: (layer, 0, nf + f)),
                  pl.BlockSpec((CONV_W, tf), lambda f: (0, f)),
                  pl.BlockSpec((1, tf), lambda f: (0, f)),
                  col, col,
                  pl.BlockSpec((None, tf, D_MODEL), lambda f: (layer, f, 0))],
        out_specs=[full, col],
        out_shape=[jax.ShapeDtypeStruct((m, D_MODEL), F32), jax.ShapeDtypeStruct((m, D_FF), F32)],
        scratch_shapes=[pltpu.VMEM((m, D_MODEL), BF16)],
        compiler_params=_cparams("arbitrary"),
        name="ffn_sample",
    )(x, g, w_up, w_up, cw, cb, p0, p1, w_down)


def _lambda_from(lam_ref, lam_init):
    v = lam_ref[...]
    a = jnp.sum(v[0:1] * v[1:2], axis=-1, keepdims=True)
    b = jnp.sum(v[2:3] * v[3:4], axis=-1, keepdims=True)
    return jnp.exp(a) - jnp.exp(b) + lam_init


def _diff_attn_prompt_kernel(qi_ref, ki_ref, slope_ref, qt_ref, k_ref, vt_ref, lam_ref, sg_ref, o_ref,
                             qs_ref, rel_ref, m_ref, l_ref, acc_ref, *, t, lam_init):
    h = pl.program_id(1)
    step = pl.program_id(2)
    qi = qi_ref[step]
    ki = ki_ref[step]
    slope = slope_ref[h]

    @pl.when(step == 0)
    def _():
        key = lax.broadcasted_iota(jnp.int32, (t, t), 0)
        qry = lax.broadcasted_iota(jnp.int32, (t, t), 1)
        rel_ref[...] = (key - qry).astype(F32) * slope

    @pl.when(ki == 0)
    def _():
        qt = qt_ref[...]
        dim = lax.broadcasted_iota(jnp.int32, qt.shape, 0)
        zero = jnp.zeros_like(qt)
        qs_ref[:, :t] = jnp.where(dim < QK_DIM, qt, zero)
        qs_ref[:, t:] = jnp.where(dim >= QK_DIM, qt, zero)
        m_ref[...] = jnp.full_like(m_ref, NEG_INF)
        l_ref[...] = jnp.zeros_like(l_ref)
        acc_ref[...] = jnp.zeros_like(acc_ref)

    def update(masked):
        s = _dot(k_ref[...], qs_ref[...])
        rel = rel_ref[...]
        if masked:
            key = lax.broadcasted_iota(jnp.int32, (t, t), 0)
            qry = lax.broadcasted_iota(jnp.int32, (t, t), 1)
            keep = key <= qry
            s = jnp.concatenate([jnp.where(keep, s[:, :t] + rel, NEG_INF),
                                 jnp.where(keep, s[:, t:] + rel, NEG_INF)], axis=1)
        else:
            s = jnp.concatenate([s[:, :t] + rel, s[:, t:] + rel], axis=1)
        off = slope * ((ki - qi) * t).astype(F32)
        m_old = m_ref[...]
        m_new = jnp.maximum(m_old, jnp.max(s, axis=0, keepdims=True) + off)
        p = jnp.exp(s - (m_new - off))
        alpha = jnp.exp(m_old - m_new)
        l_ref[...] = alpha * l_ref[...] + jnp.sum(p, axis=0, keepdims=True)
        acc_ref[...] = alpha * acc_ref[...] + _dot(vt_ref[...], p.astype(BF16))
        m_ref[...] = m_new

    @pl.when(ki < qi)
    def _():
        update(False)

    @pl.when(ki == qi)
    def _():
        update(True)
        lam = _lambda_from(lam_ref, lam_init)
        inv = 1.0 / l_ref[...]
        acc = acc_ref[...]
        o = acc[:, :t] * inv[:, :t] - lam * (acc[:, t:] * inv[:, t:])
        on = o * lax.rsqrt(jnp.mean(o * o, axis=0, keepdims=True) + EPS)
        o_ref[...] = (on.T * sg_ref[...] * (1.0 - lam_init)).astype(BF16)


def _diff_attn_prompt(qt, k, vt, slopes, lam_vecs, subln, lam_init, n_seq, seq, t):
    m = n_seq * seq
    nt = seq // t
    pairs = [(a, b) for a in range(nt) for b in range(a + 1)]
    qi = jnp.asarray([p[0] for p in pairs], jnp.int32)
    ki = jnp.asarray([p[1] for p in pairs], jnp.int32)
    qspec = pl.BlockSpec((None, None, HEAD_DIM, t), lambda b, h, s, qi, ki: (b, h, 0, qi[s]))
    kspec = pl.BlockSpec((None, None, t, HEAD_DIM), lambda b, h, s, qi, ki: (b, h, ki[s], 0))
    vspec = pl.BlockSpec((None, None, HEAD_DIM, t), lambda b, h, s, qi, ki: (b, h, 0, ki[s]))
    grid_spec = pltpu.PrefetchScalarGridSpec(
        num_scalar_prefetch=2,
        grid=(n_seq, N_HEADS, len(pairs)),
        in_specs=[pl.BlockSpec(memory_space=pltpu.SMEM), qspec, kspec, vspec,
                  pl.BlockSpec((4, QK_DIM), lambda b, h, s, qi, ki: (0, 0)),
                  pl.BlockSpec((1, HEAD_DIM), lambda b, h, s, qi, ki: (0, 0))],
        out_specs=pl.BlockSpec((t, HEAD_DIM), lambda b, h, s, qi, ki: (b * nt + qi[s], h)),
        scratch_shapes=[pltpu.VMEM((HEAD_DIM, 2 * t), BF16), pltpu.VMEM((t, t), F32),
                        pltpu.VMEM((1, 2 * t), F32), pltpu.VMEM((1, 2 * t), F32),
                        pltpu.VMEM((HEAD_DIM, 2 * t), F32)])
    return pl.pallas_call(
        functools.partial(_diff_attn_prompt_kernel, t=t, lam_init=lam_init),
        grid_spec=grid_spec,
        out_shape=jax.ShapeDtypeStruct((m, MIX_WIDTH), BF16),
        compiler_params=_cparams("parallel", "parallel", "arbitrary"),
        name="diff_attn_prompt",
    )(qi, ki, slopes, qt, k, vt, lam_vecs, subln)


MAP_ROWS = 16


PAGES_PER_STEP = 4


def _diff_attn_sample_kernel(pt_ref, qm_ref, *refs, page, past, lam_init):
    k_refs = refs[:PAGES_PER_STEP]
    v_refs = refs[PAGES_PER_STEP:2 * PAGES_PER_STEP]
    bias_ref, slope_ref, kn_ref, vn_ref, lam_ref, sg_ref, o_ref, m_ref, l_ref, acc_ref = refs[2 * PAGES_PER_STEP:]
    p = pl.program_id(1)
    n_steps = pl.num_programs(1)

    @pl.when(p == 0)
    def _():
        m_ref[...] = jnp.full_like(m_ref, NEG_INF)
        l_ref[...] = jnp.zeros_like(l_ref)
        acc_ref[...] = jnp.zeros_like(acc_ref)

    qm = qm_ref[...]
    s = jnp.concatenate(
        [jnp.einsum("hmd,hkd->hmk", qm, k_ref[...].astype(BF16), preferred_element_type=F32) for k_ref in k_refs],
        axis=-1)
    s = s + bias_ref[...]
    off = slope_ref[...] * (p * (PAGES_PER_STEP * page) - past).astype(F32)
    m_old = m_ref[...]
    m_new = jnp.maximum(m_old, jnp.max(s, axis=-1, keepdims=True) + off)
    e = jnp.exp(s - (m_new - off))
    alpha = jnp.exp(m_old - m_new)
    l_ref[...] = alpha * l_ref[...] + jnp.sum(e, axis=-1, keepdims=True)
    e = e.astype(BF16)
    pv = None
    for j, v_ref in enumerate(v_refs):
        term = jnp.einsum("hmk,hkd->hmd", e[:, :, j * page:(j + 1) * page], v_ref[...].astype(BF16),
                          preferred_element_type=F32)
        pv = term if pv is None else pv + term
    acc_ref[...] = alpha * acc_ref[...] + pv
    m_ref[...] = m_new

    @pl.when(p == n_steps - 1)
    def _():
        kn = kn_ref[...].astype(BF16).astype(F32)
        vn = vn_ref[...].astype(BF16).astype(F32)
        s_new = jnp.sum(qm.astype(F32) * kn, axis=-1, keepdims=True)
        m_old = m_ref[...]
        m_fin = jnp.maximum(m_old, s_new)
        alpha = jnp.exp(m_old - m_fin)
        e_new = jnp.exp(s_new - m_fin)
        l = alpha * l_ref[...] + e_new
        o = (alpha * acc_ref[...] + e_new * vn) / l
        lam = _lambda_from(lam_ref, lam_init)
        d = o[:, 0:1, :] - lam * o[:, 1:2, :]
        o_ref[...] = _rms(d, sg_ref[...]) * (1.0 - lam_init)


def _diff_attn_sample(page_table, qm, cache_k, cache_v, layer, bias, slope_col, kn, vn, lam_vecs, subln, lam_init):
    n_seq, n_pages = page_table.shape
    page = cache_k.shape[3]
    past = n_pages * page
    pt = page_table.reshape(-1)
    per_seq = lambda rows: pl.BlockSpec((None, N_HEADS, rows, HEAD_DIM), lambda b, p, pt: (b, 0, 0, 0))
    kv = [pl.BlockSpec((None, None, N_HEADS, page, HEAD_DIM),
                       lambda b, p, pt, j=j: (layer, pt[b * n_pages + p * PAGES_PER_STEP + j], 0, 0, 0))
          for j in range(PAGES_PER_STEP)]
    const = lambda shape: pl.BlockSpec(shape, lambda b, p, pt: (0,) * len(shape))
    stat = (N_HEADS, MAP_ROWS, 1)
    grid_spec = pltpu.PrefetchScalarGridSpec(
        num_scalar_prefetch=1,
        grid=(n_seq, n_pages // PAGES_PER_STEP),
        in_specs=[per_seq(MAP_ROWS), *kv, *kv, const((N_HEADS, MAP_ROWS, PAGES_PER_STEP * page)), const(stat),
                  per_seq(1), per_seq(1), const((4, QK_DIM)), const((1, HEAD_DIM))],
        out_specs=per_seq(1),
        scratch_shapes=[pltpu.VMEM(stat, F32), pltpu.VMEM(stat, F32),
                        pltpu.VMEM((N_HEADS, MAP_ROWS, HEAD_DIM), F32)])
    return pl.pallas_call(
        functools.partial(_diff_attn_sample_kernel, page=page, past=past, lam_init=lam_init),
        grid_spec=grid_spec,
        out_shape=jax.ShapeDtypeStruct((n_seq, N_HEADS, 1, HEAD_DIM), F32),
        compiler_params=_cparams("parallel", "arbitrary"),
        name="diff_attn_sample",
    )(pt, qm, *([cache_k] * PAGES_PER_STEP), *([cache_v] * PAGES_PER_STEP), bias, slope_col, kn, vn, lam_vecs, subln)


def kernel(x_prompt, x_sample, mem_prompt, cache_mem_k, cache_mem_v, cache_k, cache_v, state_ffn_conv, page_table, norm_mix, norm_mem, norm_ffn, w_in_a, v_norm_a, spatial_w_a, spatial_b_a, w_out_a, w_in_b, q_norm_b, k_norm_b, lambda_q1, lambda_k1, lambda_q2, lambda_k2, subln_b, w_out_b, w_mem_kv, mem_q_norm, mem_k_norm, w_up, conv_w, conv_b, w_down):
    n_seq_p, seq, _ = x_prompt.shape
    n_seq_s = x_sample.shape[0]
    depth = norm_mix.shape[0]
    n_mem = mem_prompt.shape[1]
    m_p = n_seq_p * seq
    page = cache_k.shape[2]

    xp = x_prompt.reshape(m_p, D_MODEL)
    xs = jnp.pad(x_sample.reshape(n_seq_s, D_MODEL), ((0, SAMPLE_ROWS - n_seq_s), (0, 0)))
    mem = mem_prompt.reshape(n_seq_p * n_mem, D_MODEL)
    cmk = cache_mem_k.reshape(depth, n_seq_s, n_mem, MEM_WIDTH)
    cmv = cache_mem_v.reshape(depth, n_seq_s, n_mem, MEM_WIDTH)
    ck = cache_k.transpose(0, 1, 3, 2, 4)
    cv = cache_v.transpose(0, 1, 3, 2, 4)
    conv_state = jnp.pad(state_ffn_conv, ((0, 0), (0, SAMPLE_ROWS - n_seq_s), (0, 0), (0, 0)))

    slopes = 2.0 ** (-8.0 * jnp.arange(1, N_HEADS + 1, dtype=F32) / N_HEADS)
    slope_col = jnp.broadcast_to(slopes[:, None, None], (N_HEADS, MAP_ROWS, 1))
    step_pos = jnp.arange(PAGES_PER_STEP * page, dtype=F32)
    sample_bias = jnp.broadcast_to(slopes[:, None, None] * step_pos[None, None, :],
                                   (N_HEADS, MAP_ROWS, PAGES_PER_STEP * page))
    lane = jnp.arange(HEAD_DIM)
    map_mask = jnp.stack([lane < QK_DIM, lane >= QK_DIM]).astype(F32)
    head_view = lambda a: a[:n_seq_s].reshape(n_seq_s, N_HEADS, 1, HEAD_DIM)

    tril = jnp.tril(jnp.ones((CHUNK, CHUNK), bool))
    row2 = lambda v: v.reshape(1, -1)

    mem_k_p, mem_v_p, k_p, v_p, conv_p = [], [], [], [], []
    k_s, v_s, conv_s, chunk_v_s = [], [], [], []

    w_up_b, w_down_b, w_mem_b = w_up.astype(BF16), w_down.astype(BF16), w_mem_kv.astype(BF16)
    w_in_a_b, w_out_a_b = w_in_a.astype(BF16), w_out_a.astype(BF16)
    w_in_b_b, w_out_b_b = w_in_b.astype(BF16), w_out_b.astype(BF16)

    for i in range(depth):
        mk, mv = _in_proj(
            mem, row2(norm_mem[i]), w_mem_b, i,
            [dict(width=MEM_WIDTH, kind="gnorm", gain=0, f32="flat"),
             dict(width=MEM_WIDTH, kind="plain", f32="flat")],
            [mem_k_norm[i]], MEM_HEAD_DIM, tm=n_seq_p * n_mem, tn=MEM_WIDTH)
        mem_k_p.append(mk)
        mem_v_p.append(mv)

        if i % 2 == 0:
            la = i // 2
            w_in, w_out, wl = w_in_a_b, w_out_a_b, la
            regions = [dict(width=MIX_WIDTH, kind="gelu", f32="flat"),
                       dict(width=MIX_WIDTH, kind="gelu", f32="flat"),
                       dict(width=MEM_WIDTH, kind="plain", f32="flat")]
            wt = jnp.where(tril, spatial_w_a[la], 0.0).astype(BF16)
            bs = jnp.broadcast_to(spatial_b_a[la][:, :, None], (N_GROUPS, CHUNK, CHUNK))

            u, v, qm_p = _in_proj(xp, row2(norm_mix[i]), w_in, wl, regions, [], QK_DIM, tm=1024, tn=512, x_buffers=1)
            mix_p = _mixer_a(u, v, row2(v_norm_a[la]), wt, bs, tm=512)

            u, v, qm_s = _in_proj(xs, row2(norm_mix[i]), w_in, wl, regions, [], QK_DIM, tm=SAMPLE_ROWS, tn=512)
            w0 = jnp.repeat(wt[:, 0, 0].astype(F32), CHUNK).reshape(1, MIX_WIDTH)
            b0 = jnp.repeat(spatial_b_a[la][:, 0], CHUNK).reshape(1, MIX_WIDTH)
            mix_s, vrow = _mixer_a_sample(u, v, row2(v_norm_a[la]), w0, b0)
            chunk_v_s.append(vrow[:n_seq_s])
        else:
            lb = i // 2
            lam_init = 0.8 - 0.6 * math.exp(-0.3 * i)
            w_in, w_out, wl = w_in_b_b, w_out_b_b, lb
            lam_vecs = jnp.stack([lambda_q1[lb], lambda_k1[lb], lambda_q2[lb], lambda_k2[lb]])
            subln = row2(subln_b[lb])
            gains = [q_norm_b[lb], k_norm_b[lb]]
            qk_scale = QK_DIM ** -0.5

            regions = [dict(width=MIX_WIDTH, kind="gnorm", gain=0, bf16="cols", scale=qk_scale),
                       dict(width=MIX_WIDTH, kind="gnorm", gain=1, f32="rows", bf16="rows"),
                       dict(width=MIX_WIDTH, kind="plain", f32="rows", bf16="cols"),
                       dict(width=MEM_WIDTH, kind="plain", f32="flat")]
            qt, kf, kb, vf, vt, qm_p = _in_proj(xp, row2(norm_mix[i]), w_in, wl, regions, gains, QK_DIM, tm=1024,
                                                tn=512, seq=seq, x_buffers=1)
            k_p.append(kf)
            v_p.append(vf)
            mix_p = _diff_attn_prompt(qt, kb, vt, slopes, lam_vecs, subln, lam_init, n_seq_p, seq, t=512)

            regions = [dict(width=MIX_WIDTH, kind="gnorm", gain=0, f32="flat"),
                       dict(width=MIX_WIDTH, kind="gnorm", gain=1, f32="flat"),
                       dict(width=MIX_WIDTH, kind="plain", f32="flat"),
                       dict(width=MEM_WIDTH, kind="plain", f32="flat")]
            q, kf, vf, qm_s = _in_proj(xs, row2(norm_mix[i]), w_in, wl, regions, gains, QK_DIM, tm=SAMPLE_ROWS,
                                       tn=512)
            k_s.append(kf[:n_seq_s])
            v_s.append(vf[:n_seq_s])
            qm = jnp.pad(head_view(q * qk_scale) * map_mask[None, None],
                         ((0, 0), (0, 0), (0, MAP_ROWS - 2), (0, 0))).astype(BF16)
            o = _diff_attn_sample(page_table, qm, ck, cv, lb, sample_bias, slope_col, head_view(kf), head_view(vf),
                                  lam_vecs, subln, lam_init)
            mix_s = jnp.pad(o.reshape(n_seq_s, MIX_WIDTH), ((0, SAMPLE_ROWS - n_seq_s), (0, 0))).astype(BF16)

        mo_p = _mem_attend(qm_p, row2(mem_q_norm[i]), mk, mv, rows_per_batch=seq, tm=512)
        mo_s = _mem_attend_sample(qm_s, row2(mem_q_norm[i]), cmk, cmv, i, n_seq_s)
        xp = _out_proj(mix_p, mo_p, w_out, wl, xp, tm=1024, tn=1024)
        xs = _out_proj(mix_s, mo_s, w_out, wl, xs, tm=SAMPLE_ROWS, tn=1024)

        xp, cp = _ffn_prompt(xp, row2(norm_ffn[i]), w_up_b, conv_w[i], row2(conv_b[i]), w_down_b, i, seq,
                             tm=512, tf=512)
        xs, a_s = _ffn_sample(xs, row2(norm_ffn[i]), w_up_b, conv_w[i], row2(conv_b[i]),
                              conv_state[i, :, 0], conv_state[i, :, 1], w_down_b, i, tf=512)
        conv_p.append(cp)
        conv_s.append(jnp.stack([state_ffn_conv[i, :, 1], a_s[:n_seq_s]], axis=1))

    heads = lambda a, b, t: a.reshape(b, t, N_HEADS, HEAD_DIM)
    return (xp.reshape(n_seq_p, seq, D_MODEL),
            xs[:n_seq_s].reshape(n_seq_s, 1, D_MODEL),
            jnp.stack(mem_k_p).reshape(depth, n_seq_p, n_mem, MEM_HEADS, MEM_HEAD_DIM),
            jnp.stack(mem_v_p).reshape(depth, n_seq_p, n_mem, MEM_HEADS, MEM_HEAD_DIM),
            jnp.stack(k_p).transpose(0, 1, 3, 2, 4),
            jnp.stack(v_p).transpose(0, 1, 3, 2, 4),
            jnp.stack(conv_p),
            jnp.stack([heads(a, n_seq_s, 1) for a in k_s]),
            jnp.stack([heads(a, n_seq_s, 1) for a in v_s]),
            jnp.stack(conv_s),
            jnp.stack(chunk_v_s).reshape(len(chunk_v_s), n_seq_s, 1, MIX_WIDTH))
```

```python
import functools
import math

import jax
import jax.numpy as jnp
from jax import lax
from jax.experimental import pallas as pl
from jax.experimental.pallas import tpu as pltpu

F32 = jnp.float32
BF16 = jnp.bfloat16

D_MODEL = 2048
MEM_HEADS = 4
MEM_HEAD_DIM = 128
MEM_WIDTH = MEM_HEADS * MEM_HEAD_DIM
MIX_WIDTH = D_MODEL - MEM_WIDTH
CHUNK = 128
N_GROUPS = 12
N_HEADS = 12
HEAD_DIM = 128
QK_DIM = 64
D_FF = 5632
CONV_W = 3
EPS = 1e-6
NEG_INF = -1e30
SAMPLE_ROWS = 16
MXU_WIDTH = 256

VMEM_LIMIT = 56 * 1024 * 1024


def _cparams(*sem):
    return pltpu.CompilerParams(dimension_semantics=sem, vmem_limit_bytes=VMEM_LIMIT)


def _rms(x, g):
    return x * lax.rsqrt(jnp.mean(x * x, axis=-1, keepdims=True) + EPS) * g


def _dot(a, b):
    return jnp.dot(a, b, preferred_element_type=F32)


def _dot_nt(a, b):
    return lax.dot_general(a, b, (((1,), (1,)), ((), ())), preferred_element_type=F32)


def _block_diag_ones(n, group):
    r = jnp.arange(n) // group
    return (r[:, None] == r[None, :]).astype(BF16)


def _in_proj_kernel(x_ref, g_ref, w_ref, bd_ref, *refs, regions, n_gain, group):
    gain_refs = refs[:n_gain]
    n_out = sum((r["f32"] is not None) + (r["bf16"] is not None) for r in regions)
    out_refs = refs[n_gain:n_gain + n_out]
    h_ref = refs[n_gain + n_out]
    j = pl.program_id(1)

    @pl.when(j == 0)
    def _():
        h_ref[...] = _rms(x_ref[...], g_ref[...]).astype(BF16)

    def store(ref, val, layout):
        if layout == "flat":
            ref[...] = val.astype(ref.dtype)
            return
        for hh in range(val.shape[1] // HEAD_DIM):
            head = val[:, hh * HEAD_DIM:(hh + 1) * HEAD_DIM]
            ref[hh] = head.T.astype(ref.dtype) if layout == "cols" else head.astype(ref.dtype)

    for r in regions:
        @pl.when((j >= r["start"]) & (j < r["start"] + r["ntiles"]))
        def _(r=r):
            z = _dot(h_ref[...], w_ref[...])
            if r["kind"] == "gelu":
                y = jax.nn.gelu(z)
            elif r["kind"] == "plain":
                y = z
            else:
                z2 = (z * z).astype(BF16)
                bw = bd_ref.shape[0]
                ss = jnp.concatenate([_dot(z2[:, c:c + bw], bd_ref[...]) for c in range(0, z2.shape[1], bw)], axis=1)
                y = z * lax.rsqrt(ss * (1.0 / group) + EPS) * gain_refs[r["gain"]][...]
            if r["f32"] is not None:
                store(out_refs[r["f32"]], y, r["f32_layout"])
            if r["bf16"] is not None:
                store(out_refs[r["bf16"]], y if r["scale"] == 1.0 else y * r["scale"], r["bf16_layout"])


def _in_proj(x, g, w, layer, regions, gains, group, tm, tn, seq=None, x_buffers=2):
    m, d = x.shape
    regs, out_shapes, out_specs = [], [], []
    start = 0
    for r in regions:
        nt = r["width"] // tn
        reg = dict(start=start, ntiles=nt, kind=r["kind"], gain=r.get("gain"), scale=r.get("scale", 1.0),
                   f32=None, bf16=None)
        for key, dt in (("f32", F32), ("bf16", BF16)):
            layout = r.get(key)
            reg[key + "_layout"] = layout
            if layout is None:
                continue
            reg[key] = len(out_shapes)
            tile = lambda i, j, s=start, n=nt: jnp.clip(j - s, 0, n - 1)
            if layout == "flat":
                out_shapes.append(jax.ShapeDtypeStruct((m, r["width"]), dt))
                out_specs.append(pl.BlockSpec((tm, tn), lambda i, j, tile=tile: (i, tile(i, j))))
                continue
            tps = seq // tm
            heads, hpt = r["width"] // HEAD_DIM, tn // HEAD_DIM
            if layout == "rows":
                out_shapes.append(jax.ShapeDtypeStruct((m // seq, heads, seq, HEAD_DIM), dt))
                out_specs.append(pl.BlockSpec(
                    (None, hpt, tm, HEAD_DIM), lambda i, j, tile=tile, tps=tps: (i // tps, tile(i, j), i % tps, 0)))
            else:
                out_shapes.append(jax.ShapeDtypeStruct((m // seq, heads, HEAD_DIM, seq), dt))
                out_specs.append(pl.BlockSpec(
                    (None, hpt, HEAD_DIM, tm), lambda i, j, tile=tile, tps=tps: (i // tps, tile(i, j), 0, i % tps)))
        regs.append(reg)
        start += nt
    assert start * tn == w.shape[2]
    gain_tiles = [jnp.tile(gv.reshape(1, -1), (1, tn // gv.size)) for gv in gains]
    in_specs = [
        pl.BlockSpec((tm, d), lambda i, j: (i, 0), pipeline_mode=pl.Buffered(x_buffers)),
        pl.BlockSpec((1, d), lambda i, j: (0, 0)),
        pl.BlockSpec((None, d, tn), lambda i, j: (layer, 0, j)),
        pl.BlockSpec((MXU_WIDTH, MXU_WIDTH), lambda i, j: (0, 0)),
    ] + [pl.BlockSpec((1, tn), lambda i, j: (0, 0)) for _ in gain_tiles]
    return pl.pallas_call(
        functools.partial(_in_proj_kernel, regions=regs, n_gain=len(gain_tiles), group=group),
        grid=(m // tm, start),
        in_specs=in_specs,
        out_specs=out_specs,
        out_shape=out_shapes,
        scratch_shapes=[pltpu.VMEM((tm, d), BF16)],
        compiler_params=_cparams("parallel", "arbitrary"),
        name="in_proj",
    )(x, g, w, _block_diag_ones(MXU_WIDTH, group), *gain_tiles)


def _mixer_a_kernel(u_ref, v_ref, vg_ref, wt_ref, bs_ref, o_ref, *, n_chunks):
    vb = _rms(v_ref[...], vg_ref[...]).astype(BF16)
    for g in range(N_GROUPS):
        cols = slice(g * CHUNK, (g + 1) * CHUNK)
        rhs = jnp.concatenate([vb[c * CHUNK:(c + 1) * CHUNK, cols] for c in range(n_chunks)], axis=1)
        mixed = _dot(wt_ref[g], rhs)
        for c in range(n_chunks):
            rows = slice(c * CHUNK, (c + 1) * CHUNK)
            mc = mixed[:, c * CHUNK:(c + 1) * CHUNK] + bs_ref[g]
            o_ref[rows, cols] = (u_ref[rows, cols] * mc).astype(BF16)


def _mixer_a(u, v, vgain, wt, bs, tm):
    m = u.shape[0]
    row = pl.BlockSpec((tm, MIX_WIDTH), lambda i: (i, 0))
    return pl.pallas_call(
        functools.partial(_mixer_a_kernel, n_chunks=tm // CHUNK),
        grid=(m // tm,),
        in_specs=[row, row,
                  pl.BlockSpec((1, MIX_WIDTH), lambda i: (0, 0)),
                  pl.BlockSpec((N_GROUPS, CHUNK, CHUNK), lambda i: (0, 0, 0)),
                  pl.BlockSpec((N_GROUPS, CHUNK, CHUNK), lambda i: (0, 0, 0))],
        out_specs=row,
        out_shape=jax.ShapeDtypeStruct((m, MIX_WIDTH), BF16),
        compiler_params=_cparams("parallel"),
        name="mixer_a",
    )(u, v, vgain, wt, bs)


def _mixer_a_sample_kernel(u_ref, v_ref, vg_ref, w0_ref, b0_ref, o_ref, vn_ref):
    vn = _rms(v_ref[...], vg_ref[...])
    vn_ref[...] = vn
    mixed = vn.astype(BF16).astype(F32) * w0_ref[...] + b0_ref[...]
    o_ref[...] = (u_ref[...] * mixed).astype(BF16)


def _mixer_a_sample(u, v, vgain, w0, b0):
    m = u.shape[0]
    full = pl.BlockSpec((m, MIX_WIDTH), lambda: (0, 0))
    vec = pl.BlockSpec((1, MIX_WIDTH), lambda: (0, 0))
    return pl.pallas_call(
        _mixer_a_sample_kernel,
        in_specs=[full, full, vec, vec, vec],
        out_specs=[full, full],
        out_shape=[jax.ShapeDtypeStruct((m, MIX_WIDTH), BF16), jax.ShapeDtypeStruct((m, MIX_WIDTH), F32)],
        name="mixer_a_sample",
    )(u, v, vgain, w0, b0)


def _mem_attend_tile(q, qg, k, v):
    outs = []
    for h in range(MEM_HEADS):
        cols = slice(h * MEM_HEAD_DIM, (h + 1) * MEM_HEAD_DIM)
        qn = _rms(q[:, cols], qg).astype(BF16)
        s = _dot_nt(qn, k[:, cols].astype(BF16)) * (MEM_HEAD_DIM ** -0.5)
        s = s - jnp.max(s, axis=-1, keepdims=True)
        e = jnp.exp(s)
        p = e / jnp.sum(e, axis=-1, keepdims=True)
        outs.append(_dot(p.astype(BF16), v[:, cols].astype(BF16)))
    return jnp.concatenate(outs, axis=1)


def _mem_attend_kernel(q_ref, qg_ref, k_ref, v_ref, o_ref):
    o_ref[...] = _mem_attend_tile(q_ref[...], qg_ref[...], k_ref[...], v_ref[...]).astype(BF16)


def _mem_attend(q, qg, k, v, rows_per_batch, tm):
    m = q.shape[0]
    n_mem = k.shape[0] // (m // rows_per_batch)
    tiles_per_batch = rows_per_batch // tm
    kv = pl.BlockSpec((n_mem, MEM_WIDTH), lambda i: (i // tiles_per_batch, 0))
    return pl.pallas_call(
        _mem_attend_kernel,
        grid=(m // tm,),
        in_specs=[pl.BlockSpec((tm, MEM_WIDTH), lambda i: (i, 0)),
                  pl.BlockSpec((1, MEM_HEAD_DIM), lambda i: (0, 0)), kv, kv],
        out_specs=pl.BlockSpec((tm, MEM_WIDTH), lambda i: (i, 0)),
        out_shape=jax.ShapeDtypeStruct((m, MEM_WIDTH), BF16),
        compiler_params=_cparams("parallel"),
        name="mem_attend",
    )(q, qg, k, v)


def _mem_attend_sample_kernel(q_ref, qg_ref, k_ref, v_ref, o_ref):
    b = pl.program_id(0)
    o = _mem_attend_tile(q_ref[...], qg_ref[...], k_ref[...], v_ref[...]).astype(BF16)

    @pl.when(b == 0)
    def _():
        o_ref[...] = jnp.zeros_like(o_ref)

    row = lax.broadcasted_iota(jnp.int32, o.shape, 0)
    o_ref[...] = jnp.where(row == b, o, o_ref[...])


def _mem_attend_sample(q, qg, k, v, layer, n_seq):
    m = q.shape[0]
    n_mem = k.shape[2]
    kv = pl.BlockSpec((None, None, n_mem, MEM_WIDTH), lambda b: (layer, b, 0, 0))
    full = pl.BlockSpec((m, MEM_WIDTH), lambda b: (0, 0))
    return pl.pallas_call(
        _mem_attend_sample_kernel,
        grid=(n_seq,),
        in_specs=[full, pl.BlockSpec((1, MEM_HEAD_DIM), lambda b: (0, 0)), kv, kv],
        out_specs=full,
        out_shape=jax.ShapeDtypeStruct((m, MEM_WIDTH), BF16),
        compiler_params=_cparams("arbitrary"),
        name="mem_attend_sample",
    )(q, qg, k, v)


def _out_proj_kernel(a_ref, b_ref, wa_ref, wb_ref, x_ref, o_ref):
    o_ref[...] = x_ref[...] + _dot(a_ref[...], wa_ref[...]) + _dot(b_ref[...], wb_ref[...])


def _out_proj(mix, mo, w, layer, x, tm, tn):
    m = x.shape[0]
    assert MIX_WIDTH % MEM_WIDTH == 0
    return pl.pallas_call(
        _out_proj_kernel,
        grid=(m // tm, D_MODEL // tn),
        in_specs=[pl.BlockSpec((tm, MIX_WIDTH), lambda i, j: (i, 0)),
                  pl.BlockSpec((tm, MEM_WIDTH), lambda i, j: (i, 0)),
                  pl.BlockSpec((None, MIX_WIDTH, tn), lambda i, j: (layer, 0, j)),
                  pl.BlockSpec((None, MEM_WIDTH, tn), lambda i, j: (layer, MIX_WIDTH // MEM_WIDTH, j)),
                  pl.BlockSpec((tm, tn), lambda i, j: (i, j))],
        out_specs=pl.BlockSpec((tm, tn), lambda i, j: (i, j)),
        out_shape=jax.ShapeDtypeStruct((m, D_MODEL), F32),
        compiler_params=_cparams("parallel", "arbitrary"),
        name="out_proj",
    )(mix, mo, w, w, x)


HALO = 16


def _ffn_prompt_kernel(x_ref, halo_ref, g_ref, wa_ref, wb_ref, cw_ref, cb_ref, wd_ref, o_ref, conv_ref,
                       h_ref, a_ref, *, tm, tiles_per_seq):
    i = pl.program_id(0)
    f = pl.program_id(1)

    @pl.when(f == 0)
    def _():
        h_ref[HALO:, :] = _rms(x_ref[...], g_ref[...]).astype(BF16)
        hh = _rms(halo_ref[...], g_ref[...])
        h_ref[:HALO, :] = jnp.where(i % tiles_per_seq == 0, 0.0, hh).astype(BF16)
        o_ref[...] = x_ref[...]

    a_ref[...] = _dot(h_ref[...], wa_ref[...])
    b = _dot(h_ref[HALO:, :], wb_ref[...])
    c = cb_ref[...]
    for tap in range(CONV_W):
        off = HALO - (CONV_W - 1) + tap
        c = c + a_ref[off:off + tm, :] * cw_ref[tap:tap + 1, :]
    act = (jax.nn.silu(c) * b).astype(BF16)
    half = D_MODEL // 2
    o_ref[:, :half] += _dot(act, wd_ref[:, :half])
    o_ref[:, half:] += _dot(act, wd_ref[:, half:])
    conv_ref[...] = a_ref[tm + HALO - 8:tm + HALO, :]


def _ffn_prompt(x, g, w_up, cw, cb, w_down, layer, seq, tm, tf):
    m = x.shape[0]
    nf = D_FF // tf
    tiles_per_seq = seq // tm
    halo_blocks = tm // HALO
    once = pl.Buffered(1)
    out, conv = pl.pallas_call(
        functools.partial(_ffn_prompt_kernel, tm=tm, tiles_per_seq=tiles_per_seq),
        grid=(m // tm, nf),
        in_specs=[pl.BlockSpec((tm, D_MODEL), lambda i, f: (i, 0), pipeline_mode=once),
                  pl.BlockSpec((HALO, D_MODEL), lambda i, f: (jnp.maximum(i * halo_blocks - 1, 0), 0)),
                  pl.BlockSpec((1, D_MODEL), lambda i, f: (0, 0)),
                  pl.BlockSpec((None, D_MODEL, tf), lambda i, f: (layer, 0, f)),
                  pl.BlockSpec((None, D_MODEL, tf), lambda i, f: (layer, 0, nf + f)),
                  pl.BlockSpec((CONV_W, tf), lambda i, f: (0, f)),
                  pl.BlockSpec((1, tf), lambda i, f: (0, f)),
                  pl.BlockSpec((None, tf, D_MODEL), lambda i, f: (layer, f, 0))],
        out_specs=[pl.BlockSpec((tm, D_MODEL), lambda i, f: (i, 0), pipeline_mode=once),
                   pl.BlockSpec((None, 8, tf), lambda i, f: (i, 0, f))],
        out_shape=[jax.ShapeDtypeStruct((m, D_MODEL), F32),
                   jax.ShapeDtypeStruct((m // tm, 8, D_FF), F32)],
        scratch_shapes=[pltpu.VMEM((tm + HALO, D_MODEL), BF16), pltpu.VMEM((tm + HALO, tf), F32)],
        compiler_params=_cparams("parallel", "arbitrary"),
        name="ffn_prompt",
    )(x, x, g, w_up, w_up, cw, cb, w_down)
    return out, conv[tiles_per_seq - 1::tiles_per_seq, 8 - (CONV_W - 1):, :]


def _ffn_sample_kernel(x_ref, g_ref, wa_ref, wb_ref, cw_ref, cb_ref, p0_ref, p1_ref, wd_ref, o_ref, a_out_ref,
                       h_ref):
    f = pl.program_id(0)

    @pl.when(f == 0)
    def _():
        h_ref[...] = _rms(x_ref[...], g_ref[...]).astype(BF16)

    a = _dot(h_ref[...], wa_ref[...])
    b = _dot(h_ref[...], wb_ref[...])
    a_out_ref[...] = a
    c = cb_ref[...] + p0_ref[...] * cw_ref[0:1, :] + p1_ref[...] * cw_ref[1:2, :] + a * cw_ref[2:3, :]
    y = _dot((jax.nn.silu(c) * b).astype(BF16), wd_ref[...])

    @pl.when(f == 0)
    def _():
        o_ref[...] = x_ref[...] + y

    @pl.when(f > 0)
    def _():
        o_ref[...] += y


def _ffn_sample(x, g, w_up, cw, cb, p0, p1, w_down, layer, tf):
    m = x.shape[0]
    nf = D_FF // tf
    full = pl.BlockSpec((m, D_MODEL), lambda f: (0, 0))
    col = pl.BlockSpec((m, tf), lambda f: (0, f))
    return pl.pallas_call(
        _ffn_sample_kernel,
        grid=(nf,),
        in_specs=[full,
                  pl.BlockSpec((1, D_MODEL), lambda f: (0, 0)),
                  pl.BlockSpec((None, D_MODEL, tf), lambda f: (layer, 0, f)),
                  pl.BlockSpec((None, D_MODEL, tf), lambda f: (layer, 0, nf + f)),
                  pl.BlockSpec((CONV_W, tf), lambda f: (0, f)),
                  pl.BlockSpec((1, tf), lambda f: (0, f)),
                  col, col,
                  pl.BlockSpec((None, tf, D_MODEL), lambda f: (layer, f, 0))],
        out_specs=[full, col],
        out_shape=[jax.ShapeDtypeStruct((m, D_MODEL), F32), jax.ShapeDtypeStruct((m, D_FF), F32)],
        scratch_shapes=[pltpu.VMEM((m, D_MODEL), BF16)],
        compiler_params=_cparams("arbitrary"),
        name="ffn_sample",
    )(x, g, w_up, w_up, cw, cb, p0, p1, w_down)


def _lambda_from(lam_ref, lam_init):
    v = lam_ref[...]
    a = jnp.sum(v[0:1] * v[1:2], axis=-1, keepdims=True)
    b = jnp.sum(v[2:3] * v[3:4], axis=-1, keepdims=True)
    return jnp.exp(a) - jnp.exp(b) + lam_init


AUG_ROWS = 16
POS_SPLIT = 32


def _diff_attn_prompt_kernel(qi_ref, ki_ref, slope_ref, qt_ref, k_ref, vt_ref, kfeat_ref, srows_ref, ones_ref,
                             lam_ref, sg_ref, o_ref, qs_ref, m_ref, acc_ref, *, t, lam_init):
    h = pl.program_id(1)
    step = pl.program_id(2)
    qi = qi_ref[step]
    ki = ki_ref[step]
    slope = slope_ref[h]

    @pl.when(ki == 0)
    def _():
        qt = qt_ref[...]
        dim = lax.broadcasted_iota(jnp.int32, qt.shape, 0)
        zero = jnp.zeros_like(qt)
        qs_ref[:HEAD_DIM, :t] = jnp.where(dim < QK_DIM, qt, zero)
        qs_ref[:HEAD_DIM, t:] = jnp.where(dim >= QK_DIM, qt, zero)
        qs_ref[HEAD_DIM:HEAD_DIM + AUG_ROWS, :] = srows_ref[...]
        qs_ref[HEAD_DIM + AUG_ROWS:, :] = jnp.zeros((HEAD_DIM - AUG_ROWS, 2 * t), BF16)
        m_ref[...] = jnp.full_like(m_ref, NEG_INF)
        acc_ref[...] = jnp.zeros_like(acc_ref)

    def update(masked):
        s = _dot(jnp.concatenate([k_ref[...], kfeat_ref[...]], axis=1), qs_ref[...])
        if masked:
            key = lax.broadcasted_iota(jnp.int32, (t, t), 0)
            qry = lax.broadcasted_iota(jnp.int32, (t, t), 1)
            keep = key <= qry
            s = jnp.concatenate([jnp.where(keep, s[:, :t], NEG_INF), jnp.where(keep, s[:, t:], NEG_INF)], axis=1)
        off = slope * ((ki - qi) * t).astype(F32)
        m_old = m_ref[...]
        m_new = jnp.maximum(m_old, jnp.max(s, axis=0, keepdims=True) + off)
        p = jnp.exp(s - (m_new - off)).astype(BF16)
        alpha = jnp.exp(m_old - m_new)
        vt_aug = jnp.concatenate([vt_ref[...], ones_ref[...]], axis=0)
        acc_ref[...] = alpha * acc_ref[...] + _dot(vt_aug, p)
        m_ref[...] = m_new

    @pl.when(ki < qi)
    def _():
        update(False)

    @pl.when(ki == qi)
    def _():
        update(True)
        lam = _lambda_from(lam_ref, lam_init)
        acc = acc_ref[...]
        inv = 1.0 / acc[HEAD_DIM:HEAD_DIM + 1, :]
        o = acc[:HEAD_DIM, :t] * inv[:, :t] - lam * (acc[:HEAD_DIM, t:] * inv[:, t:])
        on = o * lax.rsqrt(jnp.mean(o * o, axis=0, keepdims=True) + EPS)
        o_ref[...] = (on.T * sg_ref[...] * (1.0 - lam_init)).astype(BF16)


def _diff_attn_prompt(qt, k, vt, slopes, lam_vecs, subln, lam_init, n_seq, seq, t):
    m = n_seq * seq
    nt = seq // t
    assert t <= POS_SPLIT * POS_SPLIT
    pairs = [(a, b) for a in range(nt) for b in range(a + 1)]
    qi = jnp.asarray([p[0] for p in pairs], jnp.int32)
    ki = jnp.asarray([p[1] for p in pairs], jnp.int32)
    top16 = lambda x: lax.bitcast_convert_type(
        lax.bitcast_convert_type(x, jnp.uint32) & jnp.uint32(0xFFFF0000), F32)
    s1 = top16(slopes)
    s2 = top16(slopes - s1)
    s3 = top16(slopes - s1 - s2)
    pieces = jnp.stack([s1, s2, s3])
    rows = jnp.concatenate([pieces * POS_SPLIT, pieces, jnp.zeros((AUG_ROWS - 6, N_HEADS), F32)]).T
    srows = jnp.broadcast_to(rows[:, :, None], (N_HEADS, AUG_ROWS, 2 * t)).astype(BF16)
    pos = jnp.arange(t)
    digits = jnp.stack([pos // POS_SPLIT] * 3 + [pos % POS_SPLIT] * 3, axis=1)
    kfeat = jnp.pad(digits, ((0, 0), (0, HEAD_DIM - 6))).astype(BF16)
    ones = jnp.zeros((AUG_ROWS, t), BF16).at[0].set(1)
    const = lambda shape: pl.BlockSpec(shape, lambda b, h, s, qi, ki: (0,) * len(shape))
    qspec = pl.BlockSpec((None, None, HEAD_DIM, t), lambda b, h, s, qi, ki: (b, h, 0, qi[s]))
    kspec = pl.BlockSpec((None, None, t, HEAD_DIM), lambda b, h, s, qi, ki: (b, h, ki[s], 0))
    vspec = pl.BlockSpec((None, None, HEAD_DIM, t), lambda b, h, s, qi, ki: (b, h, 0, ki[s]))
    grid_spec = pltpu.PrefetchScalarGridSpec(
        num_scalar_prefetch=2,
        grid=(n_seq, N_HEADS, len(pairs)),
        in_specs=[pl.BlockSpec(memory_space=pltpu.SMEM), qspec, kspec, vspec,
                  const((t, HEAD_DIM)),
                  pl.BlockSpec((None, AUG_ROWS, 2 * t), lambda b, h, s, qi, ki: (h, 0, 0)),
                  const((AUG_ROWS, t)), const((4, QK_DIM)), const((1, HEAD_DIM))],
        out_specs=pl.BlockSpec((t, HEAD_DIM), lambda b, h, s, qi, ki: (b * nt + qi[s], h)),
        scratch_shapes=[pltpu.VMEM((2 * HEAD_DIM, 2 * t), BF16), pltpu.VMEM((1, 2 * t), F32),
                        pltpu.VMEM((HEAD_DIM + AUG_ROWS, 2 * t), F32)])
    return pl.pallas_call(
        functools.partial(_diff_attn_prompt_kernel, t=t, lam_init=lam_init),
        grid_spec=grid_spec,
        out_shape=jax.ShapeDtypeStruct((m, MIX_WIDTH), BF16),
        compiler_params=_cparams("parallel", "parallel", "arbitrary"),
        name="diff_attn_prompt",
    )(qi, ki, slopes, qt, k, vt, kfeat, srows, ones, lam_vecs, subln)


MAP_ROWS = 16


PAGES_PER_STEP = 8


def _diff_attn_sample_kernel(pt_ref, qm_ref, *refs, page, past, lam_init):
    k_refs = refs[:PAGES_PER_STEP]
    v_refs = refs[PAGES_PER_STEP:2 * PAGES_PER_STEP]
    bias_ref, slope_ref, kn_ref, vn_ref, lam_ref, sg_ref, o_ref, m_ref, l_ref, acc_ref = refs[2 * PAGES_PER_STEP:]
    p = pl.program_id(1)
    n_steps = pl.num_programs(1)

    @pl.when(p == 0)
    def _():
        m_ref[...] = jnp.full_like(m_ref, NEG_INF)
        l_ref[...] = jnp.zeros_like(l_ref)
        acc_ref[...] = jnp.zeros_like(acc_ref)

    qm = qm_ref[...]
    s = jnp.concatenate(
        [jnp.einsum("hmd,hkd->hmk", qm, k_ref[...].astype(BF16), preferred_element_type=F32) for k_ref in k_refs],
        axis=-1)
    s = s + bias_ref[...]
    off = slope_ref[...] * (p * (PAGES_PER_STEP * page) - past).astype(F32)
    m_old = m_ref[...]
    m_new = jnp.maximum(m_old, jnp.max(s, axis=-1, keepdims=True) + off)
    e = jnp.exp(s - (m_new - off))
    alpha = jnp.exp(m_old - m_new)
    l_ref[...] = alpha * l_ref[...] + jnp.sum(e, axis=-1, keepdims=True)
    e = e.astype(BF16)
    pv = None
    for j, v_ref in enumerate(v_refs):
        term = jnp.einsum("hmk,hkd->hmd", e[:, :, j * page:(j + 1) * page], v_ref[...].astype(BF16),
                          preferred_element_type=F32)
        pv = term if pv is None else pv + term
    acc_ref[...] = alpha * acc_ref[...] + pv
    m_ref[...] = m_new

    @pl.when(p == n_steps - 1)
    def _():
        kn = kn_ref[...].astype(BF16).astype(F32)
        vn = vn_ref[...].astype(BF16).astype(F32)
        s_new = jnp.sum(qm.astype(F32) * kn, axis=-1, keepdims=True)
        m_old = m_ref[...]
        m_fin = jnp.maximum(m_old, s_new)
        alpha = jnp.exp(m_old - m_fin)
        e_new = jnp.exp(s_new - m_fin)
        l = alpha * l_ref[...] + e_new
        o = (alpha * acc_ref[...] + e_new * vn) / l
        lam = _lambda_from(lam_ref, lam_init)
        d = o[:, 0:1, :] - lam * o[:, 1:2, :]
        o_ref[...] = _rms(d, sg_ref[...]) * (1.0 - lam_init)


def _diff_attn_sample(page_table, qm, cache_k, cache_v, layer, bias, slope_col, kn, vn, lam_vecs, subln, lam_init):
    n_seq, n_pages = page_table.shape
    page = cache_k.shape[3]
    past = n_pages * page
    pt = page_table.reshape(-1)
    per_seq = lambda rows: pl.BlockSpec((None, N_HEADS, rows, HEAD_DIM), lambda b, p, pt: (b, 0, 0, 0))
    kv = [pl.BlockSpec((None, None, N_HEADS, page, HEAD_DIM),
                       lambda b, p, pt, j=j: (layer, pt[b * n_pages + p * PAGES_PER_STEP + j], 0, 0, 0))
          for j in range(PAGES_PER_STEP)]
    const = lambda shape: pl.BlockSpec(shape, lambda b, p, pt: (0,) * len(shape))
    stat = (N_HEADS, MAP_ROWS, 1)
    grid_spec = pltpu.PrefetchScalarGridSpec(
        num_scalar_prefetch=1,
        grid=(n_seq, n_pages // PAGES_PER_STEP),
        in_specs=[per_seq(MAP_ROWS), *kv, *kv, const((N_HEADS, MAP_ROWS, PAGES_PER_STEP * page)), const(stat),
                  per_seq(1), per_seq(1), const((4, QK_DIM)), const((1, HEAD_DIM))],
        out_specs=per_seq(1),
        scratch_shapes=[pltpu.VMEM(stat, F32), pltpu.VMEM(stat, F32),
                        pltpu.VMEM((N_HEADS, MAP_ROWS, HEAD_DIM), F32)])
    return pl.pallas_call(
        functools.partial(_diff_attn_sample_kernel, page=page, past=past, lam_init=lam_init),
        grid_spec=grid_spec,
        out_shape=jax.ShapeDtypeStruct((n_seq, N_HEADS, 1, HEAD_DIM), F32),
        compiler_params=_cparams("parallel", "arbitrary"),
        name="diff_attn_sample",
    )(pt, qm, *([cache_k] * PAGES_PER_STEP), *([cache_v] * PAGES_PER_STEP), bias, slope_col, kn, vn, lam_vecs, subln)


def kernel(x_prompt, x_sample, mem_prompt, cache_mem_k, cache_mem_v, cache_k, cache_v, state_ffn_conv, page_table, norm_mix, norm_mem, norm_ffn, w_in_a, v_norm_a, spatial_w_a, spatial_b_a, w_out_a, w_in_b, q_norm_b, k_norm_b, lambda_q1, lambda_k1, lambda_q2, lambda_k2, subln_b, w_out_b, w_mem_kv, mem_q_norm, mem_k_norm, w_up, conv_w, conv_b, w_down):
    n_seq_p, seq, _ = x_prompt.shape
    n_seq_s = x_sample.shape[0]
    depth = norm_mix.shape[0]
    n_mem = mem_prompt.shape[1]
    m_p = n_seq_p * seq
    page = cache_k.shape[2]

    xp = x_prompt.reshape(m_p, D_MODEL)
    xs = jnp.pad(x_sample.reshape(n_seq_s, D_MODEL), ((0, SAMPLE_ROWS - n_seq_s), (0, 0)))
    mem = mem_prompt.reshape(n_seq_p * n_mem, D_MODEL)
    cmk = cache_mem_k.reshape(depth, n_seq_s, n_mem, MEM_WIDTH)
    cmv = cache_mem_v.reshape(depth, n_seq_s, n_mem, MEM_WIDTH)
    ck = cache_k.transpose(0, 1, 3, 2, 4)
    cv = cache_v.transpose(0, 1, 3, 2, 4)
    conv_state = jnp.pad(state_ffn_conv, ((0, 0), (0, SAMPLE_ROWS - n_seq_s), (0, 0), (0, 0)))

    slopes = 2.0 ** (-8.0 * jnp.arange(1, N_HEADS + 1, dtype=F32) / N_HEADS)
    slope_col = jnp.broadcast_to(slopes[:, None, None], (N_HEADS, MAP_ROWS, 1))
    step_pos = jnp.arange(PAGES_PER_STEP * page, dtype=F32)
    sample_bias = jnp.broadcast_to(slopes[:, None, None] * step_pos[None, None, :],
                                   (N_HEADS, MAP_ROWS, PAGES_PER_STEP * page))
    lane = jnp.arange(HEAD_DIM)
    map_mask = jnp.stack([lane < QK_DIM, lane >= QK_DIM]).astype(F32)
    head_view = lambda a: a[:n_seq_s].reshape(n_seq_s, N_HEADS, 1, HEAD_DIM)

    tril = jnp.tril(jnp.ones((CHUNK, CHUNK), bool))
    row2 = lambda v: v.reshape(1, -1)

    mem_k_p, mem_v_p, k_p, v_p, conv_p = [], [], [], [], []
    k_s, v_s, conv_s, chunk_v_s = [], [], [], []

    w_up_b, w_down_b, w_mem_b = w_up.astype(BF16), w_down.astype(BF16), w_mem_kv.astype(BF16)
    w_in_a_b, w_out_a_b = w_in_a.astype(BF16), w_out_a.astype(BF16)
    w_in_b_b, w_out_b_b = w_in_b.astype(BF16), w_out_b.astype(BF16)

    for i in range(depth):
        mk, mv = _in_proj(
            mem, row2(norm_mem[i]), w_mem_b, i,
            [dict(width=MEM_WIDTH, kind="gnorm", gain=0, f32="flat"),
             dict(width=MEM_WIDTH, kind="plain", f32="flat")],
            [mem_k_norm[i]], MEM_HEAD_DIM, tm=n_seq_p * n_mem, tn=MEM_WIDTH)
        mem_k_p.append(mk)
        mem_v_p.append(mv)

        if i % 2 == 0:
            la = i // 2
            w_in, w_out, wl = w_in_a_b, w_out_a_b, la
            regions = [dict(width=MIX_WIDTH, kind="gelu", f32="flat"),
                       dict(width=MIX_WIDTH, kind="gelu", f32="flat"),
                       dict(width=MEM_WIDTH, kind="plain", f32="flat")]
            wt = jnp.where(tril, spatial_w_a[la], 0.0).astype(BF16)
            bs = jnp.broadcast_to(spatial_b_a[la][:, :, None], (N_GROUPS, CHUNK, CHUNK))

            u, v, qm_p = _in_proj(xp, row2(norm_mix[i]), w_in, wl, regions, [], QK_DIM, tm=1024, tn=512, x_buffers=1)
            mix_p = _mixer_a(u, v, row2(v_norm_a[la]), wt, bs, tm=512)

            u, v, qm_s = _in_proj(xs, row2(norm_mix[i]), w_in, wl, regions, [], QK_DIM, tm=SAMPLE_ROWS, tn=512)
            w0 = jnp.repeat(wt[:, 0, 0].astype(F32), CHUNK).reshape(1, MIX_WIDTH)
            b0 = jnp.repeat(spatial_b_a[la][:, 0], CHUNK).reshape(1, MIX_WIDTH)
            mix_s, vrow = _mixer_a_sample(u, v, row2(v_norm_a[la]), w0, b0)
            chunk_v_s.append(vrow[:n_seq_s])
        else:
            lb = i // 2
            lam_init = 0.8 - 0.6 * math.exp(-0.3 * i)
            w_in, w_out, wl = w_in_b_b, w_out_b_b, lb
            lam_vecs = jnp.stack([lambda_q1[lb], lambda_k1[lb], lambda_q2[lb], lambda_k2[lb]])
            subln = row2(subln_b[lb])
            gains = [q_norm_b[lb], k_norm_b[lb]]
            qk_scale = QK_DIM ** -0.5

            regions = [dict(width=MIX_WIDTH, kind="gnorm", gain=0, bf16="cols", scale=qk_scale),
                       dict(width=MIX_WIDTH, kind="gnorm", gain=1, f32="rows", bf16="rows"),
                       dict(width=MIX_WIDTH, kind="plain", f32="rows", bf16="cols"),
                       dict(width=MEM_WIDTH, kind="plain", f32="flat")]
            qt, kf, kb, vf, vt, qm_p = _in_proj(xp, row2(norm_mix[i]), w_in, wl, regions, gains, QK_DIM, tm=1024,
                                                tn=512, seq=seq, x_buffers=1)
            k_p.append(kf)
            v_p.append(vf)
            mix_p = _diff_attn_prompt(qt, kb, vt, slopes, lam_vecs, subln, lam_init, n_seq_p, seq, t=1024)

            regions = [dict(width=MIX_WIDTH, kind="gnorm", gain=0, f32="flat"),
                       dict(width=MIX_WIDTH, kind="gnorm", gain=1, f32="flat"),
                       dict(width=MIX_WIDTH, kind="plain", f32="flat"),
                       dict(width=MEM_WIDTH, kind="plain", f32="flat")]
            q, kf, vf, qm_s = _in_proj(xs, row2(norm_mix[i]), w_in, wl, regions, gains, QK_DIM, tm=SAMPLE_ROWS,
                                       tn=512)
            k_s.append(kf[:n_seq_s])
            v_s.append(vf[:n_seq_s])
            qm = jnp.pad(head_view(q * qk_scale) * map_mask[None, None],
                         ((0, 0), (0, 0), (0, MAP_ROWS - 2), (0, 0))).astype(BF16)
            o = _diff_attn_sample(page_table, qm, ck, cv, lb, sample_bias, slope_col, head_view(kf), head_view(vf),
                                  lam_vecs, subln, lam_init)
            mix_s = jnp.pad(o.reshape(n_seq_s, MIX_WIDTH), ((0, SAMPLE_ROWS - n_seq_s), (0, 0))).astype(BF16)

        mo_p = _mem_attend(qm_p, row2(mem_q_norm[i]), mk, mv, rows_per_batch=seq, tm=512)
        mo_s = _mem_attend_sample(qm_s, row2(mem_q_norm[i]), cmk, cmv, i, n_seq_s)
        xp = _out_proj(mix_p, mo_p, w_out, wl, xp, tm=1024, tn=1024)
        xs = _out_proj(mix_s, mo_s, w_out, wl, xs, tm=SAMPLE_ROWS, tn=1024)

        xp, cp = _ffn_prompt(xp, row2(norm_ffn[i]), w_up_b, conv_w[i], row2(conv_b[i]), w_down_b, i, seq,
                             tm=1024, tf=512)
        xs, a_s = _ffn_sample(xs, row2(norm_ffn[i]), w_up_b, conv_w[i], row2(conv_b[i]),
                              conv_state[i, :, 0], conv_state[i, :, 1], w_down_b, i, tf=512)
        conv_p.append(cp)
        conv_s.append(jnp.stack([state_ffn_conv[i, :, 1], a_s[:n_seq_s]], axis=1))

    heads = lambda a, b, t: a.reshape(b, t, N_HEADS, HEAD_DIM)
    return (xp.reshape(n_seq_p, seq, D_MODEL),
            xs[:n_seq_s].reshape(n_seq_s, 1, D_MODEL),
            jnp.stack(mem_k_p).reshape(depth, n_seq_p, n_mem, MEM_HEADS, MEM_HEAD_DIM),
            jnp.stack(mem_v_p).reshape(depth, n_seq_p, n_mem, MEM_HEADS, MEM_HEAD_DIM),
            jnp.stack(k_p).transpose(0, 1, 3, 2, 4),
            jnp.stack(v_p).transpose(0, 1, 3, 2, 4),
            jnp.stack(conv_p),
            jnp.stack([heads(a, n_seq_s, 1) for a in k_s]),
            jnp.stack([heads(a, n_seq_s, 1) for a in v_s]),
            jnp.stack(conv_s),
            jnp.stack(chunk_v_s).reshape(len(chunk_v_s), n_seq_s, 1, MIX_WIDTH))
```

```python
import functools
import math

import jax
import jax.numpy as jnp
from jax import lax
from jax.experimental import pallas as pl
from jax.experimental.pallas import tpu as pltpu

F32 = jnp.float32
BF16 = jnp.bfloat16

D_MODEL = 2048
MEM_HEADS = 4
MEM_HEAD_DIM = 128
MEM_WIDTH = MEM_HEADS * MEM_HEAD_DIM
MIX_WIDTH = D_MODEL - MEM_WIDTH
CHUNK = 128
N_GROUPS = 12
N_HEADS = 12
HEAD_DIM = 128
QK_DIM = 64
D_FF = 5632
CONV_W = 3
EPS = 1e-6
NEG_INF = -1e30
SAMPLE_ROWS = 16
MXU_WIDTH = 256

VMEM_LIMIT = 56 * 1024 * 1024


def _cparams(*sem):
    return pltpu.CompilerParams(dimension_semantics=sem, vmem_limit_bytes=VMEM_LIMIT)


def _rms(x, g):
    return x * lax.rsqrt(jnp.mean(x * x, axis=-1, keepdims=True) + EPS) * g


def _dot(a, b):
    return jnp.dot(a, b, preferred_element_type=F32)


def _dot_nt(a, b):
    return lax.dot_general(a, b, (((1,), (1,)), ((), ())), preferred_element_type=F32)


def _block_diag_ones(n, group):
    r = jnp.arange(n) // group
    return (r[:, None] == r[None, :]).astype(BF16)


def _in_proj_kernel(x_ref, g_ref, w_ref, bd_ref, *refs, regions, n_gain, group):
    gain_refs = refs[:n_gain]
    n_out = sum((r["f32"] is not None) + (r["bf16"] is not None) for r in regions)
    out_refs = refs[n_gain:n_gain + n_out]
    h_ref = refs[n_gain + n_out]
    j = pl.program_id(1)

    @pl.when(j == 0)
    def _():
        h_ref[...] = _rms(x_ref[...], g_ref[...]).astype(BF16)

    def store(ref, val, layout):
        if layout == "flat":
            ref[...] = val.astype(ref.dtype)
            return
        for hh in range(val.shape[1] // HEAD_DIM):
            head = val[:, hh * HEAD_DIM:(hh + 1) * HEAD_DIM]
            ref[hh] = head.T.astype(ref.dtype) if layout == "cols" else head.astype(ref.dtype)

    for r in regions:
        @pl.when((j >= r["start"]) & (j < r["start"] + r["ntiles"]))
        def _(r=r):
            z = _dot(h_ref[...], w_ref[...])
            if r["kind"] == "gelu":
                y = jax.nn.gelu(z)
            elif r["kind"] == "plain":
                y = z
            else:
                z2 = (z * z).astype(BF16)
                bw = bd_ref.shape[0]
                ss = jnp.concatenate([_dot(z2[:, c:c + bw], bd_ref[...]) for c in range(0, z2.shape[1], bw)], axis=1)
                y = z * lax.rsqrt(ss * (1.0 / group) + EPS) * gain_refs[r["gain"]][...]
            if r["f32"] is not None:
                store(out_refs[r["f32"]], y, r["f32_layout"])
            if r["bf16"] is not None:
                store(out_refs[r["bf16"]], y if r["scale"] == 1.0 else y * r["scale"], r["bf16_layout"])


def _in_proj(x, g, w, layer, regions, gains, group, tm, tn, seq=None, x_buffers=2):
    m, d = x.shape
    regs, out_shapes, out_specs = [], [], []
    start = 0
    for r in regions:
        nt = r["width"] // tn
        reg = dict(start=start, ntiles=nt, kind=r["kind"], gain=r.get("gain"), scale=r.get("scale", 1.0),
                   f32=None, bf16=None)
        for key, dt in (("f32", F32), ("bf16", BF16)):
            layout = r.get(key)
            reg[key + "_layout"] = layout
            if layout is None:
                continue
            reg[key] = len(out_shapes)
            tile = lambda i, j, s=start, n=nt: jnp.clip(j - s, 0, n - 1)
            if layout == "flat":
                out_shapes.append(jax.ShapeDtypeStruct((m, r["width"]), dt))
                out_specs.append(pl.BlockSpec((tm, tn), lambda i, j, tile=tile: (i, tile(i, j))))
                continue
            tps = seq // tm
            heads, hpt = r["width"] // HEAD_DIM, tn // HEAD_DIM
            if layout == "rows":
                out_shapes.append(jax.ShapeDtypeStruct((m // seq, heads, seq, HEAD_DIM), dt))
                out_specs.append(pl.BlockSpec(
                    (None, hpt, tm, HEAD_DIM), lambda i, j, tile=tile, tps=tps: (i // tps, tile(i, j), i % tps, 0)))
            else:
                out_shapes.append(jax.ShapeDtypeStruct((m // seq, heads, HEAD_DIM, seq), dt))
                out_specs.append(pl.BlockSpec(
                    (None, hpt, HEAD_DIM, tm), lambda i, j, tile=tile, tps=tps: (i // tps, tile(i, j), 0, i % tps)))
        regs.append(reg)
        start += nt
    assert start * tn == w.shape[2]
    gain_tiles = [jnp.tile(gv.reshape(1, -1), (1, tn // gv.size)) for gv in gains]
    in_specs = [
        pl.BlockSpec((tm, d), lambda i, j: (i, 0), pipeline_mode=pl.Buffered(x_buffers)),
        pl.BlockSpec((1, d), lambda i, j: (0, 0)),
        pl.BlockSpec((None, d, tn), lambda i, j: (layer, 0, j)),
        pl.BlockSpec((MXU_WIDTH, MXU_WIDTH), lambda i, j: (0, 0)),
    ] + [pl.BlockSpec((1, tn), lambda i, j: (0, 0)) for _ in gain_tiles]
    return pl.pallas_call(
        functools.partial(_in_proj_kernel, regions=regs, n_gain=len(gain_tiles), group=group),
        grid=(m // tm, start),
        in_specs=in_specs,
        out_specs=out_specs,
        out_shape=out_shapes,
        scratch_shapes=[pltpu.VMEM((tm, d), BF16)],
        compiler_params=_cparams("parallel", "arbitrary"),
        name="in_proj",
    )(x, g, w, _block_diag_ones(MXU_WIDTH, group), *gain_tiles)


def _mixer_a_kernel(u_ref, v_ref, vg_ref, wt_ref, bs_ref, o_ref, *, n_chunks):
    vb = _rms(v_ref[...], vg_ref[...]).astype(BF16)
    for g in range(N_GROUPS):
        cols = slice(g * CHUNK, (g + 1) * CHUNK)
        rhs = jnp.concatenate([vb[c * CHUNK:(c + 1) * CHUNK, cols] for c in range(n_chunks)], axis=1)
        mixed = _dot(wt_ref[g], rhs)
        for c in range(n_chunks):
            rows = slice(c * CHUNK, (c + 1) * CHUNK)
            mc = mixed[:, c * CHUNK:(c + 1) * CHUNK] + bs_ref[g]
            o_ref[rows, cols] = (u_ref[rows, cols] * mc).astype(BF16)


def _mixer_a(u, v, vgain, wt, bs, tm):
    m = u.shape[0]
    row = pl.BlockSpec((tm, MIX_WIDTH), lambda i: (i, 0))
    return pl.pallas_call(
        functools.partial(_mixer_a_kernel, n_chunks=tm // CHUNK),
        grid=(m // tm,),
        in_specs=[row, row,
                  pl.BlockSpec((1, MIX_WIDTH), lambda i: (0, 0)),
                  pl.BlockSpec((N_GROUPS, CHUNK, CHUNK), lambda i: (0, 0, 0)),
                  pl.BlockSpec((N_GROUPS, CHUNK, CHUNK), lambda i: (0, 0, 0))],
        out_specs=row,
        out_shape=jax.ShapeDtypeStruct((m, MIX_WIDTH), BF16),
        compiler_params=_cparams("parallel"),
        name="mixer_a",
    )(u, v, vgain, wt, bs)


def _mixer_a_sample_kernel(u_ref, v_ref, vg_ref, w0_ref, b0_ref, o_ref, vn_ref):
    vn = _rms(v_ref[...], vg_ref[...])
    vn_ref[...] = vn
    mixed = vn.astype(BF16).astype(F32) * w0_ref[...] + b0_ref[...]
    o_ref[...] = (u_ref[...] * mixed).astype(BF16)


def _mixer_a_sample(u, v, vgain, w0, b0):
    m = u.shape[0]
    full = pl.BlockSpec((m, MIX_WIDTH), lambda: (0, 0))
    vec = pl.BlockSpec((1, MIX_WIDTH), lambda: (0, 0))
    return pl.pallas_call(
        _mixer_a_sample_kernel,
        in_specs=[full, full, vec, vec, vec],
        out_specs=[full, full],
        out_shape=[jax.ShapeDtypeStruct((m, MIX_WIDTH), BF16), jax.ShapeDtypeStruct((m, MIX_WIDTH), F32)],
        name="mixer_a_sample",
    )(u, v, vgain, w0, b0)


def _mem_attend_tile(q, qg, k, v):
    outs = []
    for h in range(MEM_HEADS):
        cols = slice(h * MEM_HEAD_DIM, (h + 1) * MEM_HEAD_DIM)
        qn = _rms(q[:, cols], qg).astype(BF16)
        s = _dot_nt(qn, k[:, cols].astype(BF16)) * (MEM_HEAD_DIM ** -0.5)
        s = s - jnp.max(s, axis=-1, keepdims=True)
        e = jnp.exp(s)
        p = e / jnp.sum(e, axis=-1, keepdims=True)
        outs.append(_dot(p.astype(BF16), v[:, cols].astype(BF16)))
    return jnp.concatenate(outs, axis=1)


def _mem_attend_kernel(q_ref, qg_ref, k_ref, v_ref, o_ref):
    o_ref[...] = _mem_attend_tile(q_ref[...], qg_ref[...], k_ref[...], v_ref[...]).astype(BF16)


def _mem_attend(q, qg, k, v, rows_per_batch, tm):
    m = q.shape[0]
    n_mem = k.shape[0] // (m // rows_per_batch)
    tiles_per_batch = rows_per_batch // tm
    kv = pl.BlockSpec((n_mem, MEM_WIDTH), lambda i: (i // tiles_per_batch, 0))
    return pl.pallas_call(
        _mem_attend_kernel,
        grid=(m // tm,),
        in_specs=[pl.BlockSpec((tm, MEM_WIDTH), lambda i: (i, 0)),
                  pl.BlockSpec((1, MEM_HEAD_DIM), lambda i: (0, 0)), kv, kv],
        out_specs=pl.BlockSpec((tm, MEM_WIDTH), lambda i: (i, 0)),
        out_shape=jax.ShapeDtypeStruct((m, MEM_WIDTH), BF16),
        compiler_params=_cparams("parallel"),
        name="mem_attend",
    )(q, qg, k, v)


def _mem_attend_sample_kernel(q_ref, qg_ref, k_ref, v_ref, o_ref):
    b = pl.program_id(0)
    o = _mem_attend_tile(q_ref[...], qg_ref[...], k_ref[...], v_ref[...]).astype(BF16)

    @pl.when(b == 0)
    def _():
        o_ref[...] = jnp.zeros_like(o_ref)

    row = lax.broadcasted_iota(jnp.int32, o.shape, 0)
    o_ref[...] = jnp.where(row == b, o, o_ref[...])


def _mem_attend_sample(q, qg, k, v, layer, n_seq):
    m = q.shape[0]
    n_mem = k.shape[2]
    kv = pl.BlockSpec((None, None, n_mem, MEM_WIDTH), lambda b: (layer, b, 0, 0))
    full = pl.BlockSpec((m, MEM_WIDTH), lambda b: (0, 0))
    return pl.pallas_call(
        _mem_attend_sample_kernel,
        grid=(n_seq,),
        in_specs=[full, pl.BlockSpec((1, MEM_HEAD_DIM), lambda b: (0, 0)), kv, kv],
        out_specs=full,
        out_shape=jax.ShapeDtypeStruct((m, MEM_WIDTH), BF16),
        compiler_params=_cparams("arbitrary"),
        name="mem_attend_sample",
    )(q, qg, k, v)


def _out_proj_kernel(a_ref, b_ref, wa_ref, wb_ref, x_ref, o_ref):
    o_ref[...] = x_ref[...] + _dot(a_ref[...], wa_ref[...]) + _dot(b_ref[...], wb_ref[...])


def _out_proj(mix, mo, w, layer, x, tm, tn):
    m = x.shape[0]
    assert MIX_WIDTH % MEM_WIDTH == 0
    return pl.pallas_call(
        _out_proj_kernel,
        grid=(m // tm, D_MODEL // tn),
        in_specs=[pl.BlockSpec((tm, MIX_WIDTH), lambda i, j: (i, 0)),
                  pl.BlockSpec((tm, MEM_WIDTH), lambda i, j: (i, 0)),
                  pl.BlockSpec((None, MIX_WIDTH, tn), lambda i, j: (layer, 0, j)),
                  pl.BlockSpec((None, MEM_WIDTH, tn), lambda i, j: (layer, MIX_WIDTH // MEM_WIDTH, j)),
                  pl.BlockSpec((tm, tn), lambda i, j: (i, j))],
        out_specs=pl.BlockSpec((tm, tn), lambda i, j: (i, j)),
        out_shape=jax.ShapeDtypeStruct((m, D_MODEL), F32),
        compiler_params=_cparams("parallel", "arbitrary"),
        name="out_proj",
    )(mix, mo, w, w, x)


HALO = 16


def _ffn_prompt_kernel(x_ref, halo_ref, g_ref, wa_ref, wb_ref, cw_ref, cb_ref, wd_ref, o_ref, conv_ref,
                       h_ref, a_ref, *, tm, tiles_per_seq):
    i = pl.program_id(0)
    f = pl.program_id(1)

    @pl.when(f == 0)
    def _():
        h_ref[HALO:, :] = _rms(x_ref[...], g_ref[...]).astype(BF16)
        hh = _rms(halo_ref[...], g_ref[...])
        h_ref[:HALO, :] = jnp.where(i % tiles_per_seq == 0, 0.0, hh).astype(BF16)
        o_ref[...] = x_ref[...]

    a_ref[...] = _dot(h_ref[...], wa_ref[...])
    b = _dot(h_ref[HALO:, :], wb_ref[...])
    c = cb_ref[...]
    for tap in range(CONV_W):
        off = HALO - (CONV_W - 1) + tap
        c = c + a_ref[off:off + tm, :] * cw_ref[tap:tap + 1, :]
    act = (jax.nn.silu(c) * b).astype(BF16)
    half = D_MODEL // 2
    o_ref[:, :half] += _dot(act, wd_ref[:, :half])
    o_ref[:, half:] += _dot(act, wd_ref[:, half:])
    conv_ref[...] = a_ref[tm + HALO - 8:tm + HALO, :]


def _ffn_prompt(x, g, w_up, cw, cb, w_down, layer, seq, tm, tf):
    m = x.shape[0]
    nf = D_FF // tf
    tiles_per_seq = seq // tm
    halo_blocks = tm // HALO
    once = pl.Buffered(1)
    out, conv = pl.pallas_call(
        functools.partial(_ffn_prompt_kernel, tm=tm, tiles_per_seq=tiles_per_seq),
        grid=(m // tm, nf),
        in_specs=[pl.BlockSpec((tm, D_MODEL), lambda i, f: (i, 0)),
                  pl.BlockSpec((HALO, D_MODEL), lambda i, f: (jnp.maximum(i * halo_blocks - 1, 0), 0)),
                  pl.BlockSpec((1, D_MODEL), lambda i, f: (0, 0)),
                  pl.BlockSpec((None, D_MODEL, tf), lambda i, f: (layer, 0, f)),
                  pl.BlockSpec((None, D_MODEL, tf), lambda i, f: (layer, 0, nf + f)),
                  pl.BlockSpec((CONV_W, tf), lambda i, f: (0, f)),
                  pl.BlockSpec((1, tf), lambda i, f: (0, f)),
                  pl.BlockSpec((None, tf, D_MODEL), lambda i, f: (layer, f, 0))],
        out_specs=[pl.BlockSpec((tm, D_MODEL), lambda i, f: (i, 0), pipeline_mode=once),
                   pl.BlockSpec((None, 8, tf), lambda i, f: (i, 0, f))],
        out_shape=[jax.ShapeDtypeStruct((m, D_MODEL), F32),
                   jax.ShapeDtypeStruct((m // tm, 8, D_FF), F32)],
        scratch_shapes=[pltpu.VMEM((tm + HALO, D_MODEL), BF16), pltpu.VMEM((tm + HALO, tf), F32)],
        compiler_params=_cparams("parallel", "arbitrary"),
        name="ffn_prompt",
    )(x, x, g, w_up, w_up, cw, cb, w_down)
    return out, conv[tiles_per_seq - 1::tiles_per_seq, 8 - (CONV_W - 1):, :]


def _ffn_sample_kernel(x_ref, g_ref, wa_ref, wb_ref, cw_ref, cb_ref, p0_ref, p1_ref, wd_ref, o_ref, a_out_ref,
                       h_ref):
    f = pl.program_id(0)

    @pl.when(f == 0)
    def _():
        h_ref[...] = _rms(x_ref[...], g_ref[...]).astype(BF16)

    a = _dot(h_ref[...], wa_ref[...])
    b = _dot(h_ref[...], wb_ref[...])
    a_out_ref[...] = a
    c = cb_ref[...] + p0_ref[...] * cw_ref[0:1, :] + p1_ref[...] * cw_ref[1:2, :] + a * cw_ref[2:3, :]
    y = _dot((jax.nn.silu(c) * b).astype(BF16), wd_ref[...])

    @pl.when(f == 0)
    def _():
        o_ref[...] = x_ref[...] + y

    @pl.when(f > 0)
    def _():
        o_ref[...] += y


def _ffn_sample(x, g, w_up, cw, cb, p0, p1, w_down, layer, tf):
    m = x.shape[0]
    nf = D_FF // tf
    full = pl.BlockSpec((m, D_MODEL), lambda f: (0, 0))
    col = pl.BlockSpec((m, tf), lambda f: (0, f))
    return pl.pallas_call(
        _ffn_sample_kernel,
        grid=(nf,),
        in_specs=[full,
                  pl.BlockSpec((1, D_MODEL), lambda f: (0, 0)),
                  pl.BlockSpec((None, D_MODEL, tf), lambda f: (layer, 0, f)),
                  pl.BlockSpec((None, D_MODEL, tf), lambda f: (layer, 0, nf + f)),
                  pl.BlockSpec((CONV_W, tf), lambda f: (0, f)),
                  pl.BlockSpec((1, tf), lambda f: (0, f)),
                  col, col,
                  pl.BlockSpec((None, tf, D_MODEL), lambda f: (layer, f, 0))],
        out_specs=[full, col],
        out_shape=[jax.ShapeDtypeStruct((m, D_MODEL), F32), jax.ShapeDtypeStruct((m, D_FF), F32)],
        scratch_shapes=[pltpu.VMEM((m, D_MODEL), BF16)],
        compiler_params=_cparams("arbitrary"),
        name="ffn_sample",
    )(x, g, w_up, w_up, cw, cb, p0, p1, w_down)


def _lambda_from(lam_ref, lam_init):
    v = lam_ref[...]
    a = jnp.sum(v[0:1] * v[1:2], axis=-1, keepdims=True)
    b = jnp.sum(v[2:3] * v[3:4], axis=-1, keepdims=True)
    return jnp.exp(a) - jnp.exp(b) + lam_init


HEADS_PER_STEP = 4
AUG_ROWS = 16
POS_SPLIT = 32


def _diff_attn_prompt_kernel(qi_ref, ki_ref, slope_ref, qt_ref, k_ref, vt_ref, kfeat_ref, srows_ref, ones_ref,
                             lam_ref, sg_ref, o_ref, qs_ref, m_ref, acc_ref, *, t, lam_init):
    hp = pl.program_id(1)
    step = pl.program_id(2)
    qi = qi_ref[step]
    ki = ki_ref[step]
    heads = range(HEADS_PER_STEP)

    @pl.when(ki == 0)
    def _():
        for hh in heads:
            qt = qt_ref[hh]
            dim = lax.broadcasted_iota(jnp.int32, qt.shape, 0)
            zero = jnp.zeros_like(qt)
            qs_ref[hh, :HEAD_DIM, :t] = jnp.where(dim < QK_DIM, qt, zero)
            qs_ref[hh, :HEAD_DIM, t:] = jnp.where(dim >= QK_DIM, qt, zero)
            qs_ref[hh, HEAD_DIM:HEAD_DIM + AUG_ROWS, :] = srows_ref[hh]
            qs_ref[hh, HEAD_DIM + AUG_ROWS:, :] = jnp.zeros((HEAD_DIM - AUG_ROWS, 2 * t), BF16)
        m_ref[...] = jnp.full_like(m_ref, NEG_INF)
        acc_ref[...] = jnp.zeros_like(acc_ref)

    def update(masked):
        for hh in heads:
            slope = slope_ref[hp * HEADS_PER_STEP + hh]
            s = _dot(jnp.concatenate([k_ref[hh], kfeat_ref[...]], axis=1), qs_ref[hh])
            if masked:
                key = lax.broadcasted_iota(jnp.int32, (t, t), 0)
                qry = lax.broadcasted_iota(jnp.int32, (t, t), 1)
                keep = key <= qry
                s = jnp.concatenate([jnp.where(keep, s[:, :t], NEG_INF), jnp.where(keep, s[:, t:], NEG_INF)],
                                    axis=1)
            off = slope * ((ki - qi) * t).astype(F32)
            m_old = m_ref[hh]
            m_new = jnp.maximum(m_old, jnp.max(s, axis=0, keepdims=True) + off)
            p = jnp.exp(s - (m_new - off)).astype(BF16)
            alpha = jnp.exp(m_old - m_new)
            vt_aug = jnp.concatenate([vt_ref[hh], ones_ref[...]], axis=0)
            acc_ref[hh] = alpha * acc_ref[hh] + _dot(vt_aug, p)
            m_ref[hh] = m_new

    @pl.when(ki < qi)
    def _():
        update(False)

    @pl.when(ki == qi)
    def _():
        update(True)
        lam = _lambda_from(lam_ref, lam_init)
        for hh in heads:
            acc = acc_ref[hh]
            inv = 1.0 / acc[HEAD_DIM:HEAD_DIM + 1, :]
            o = acc[:HEAD_DIM, :t] * inv[:, :t] - lam * (acc[:HEAD_DIM, t:] * inv[:, t:])
            on = o * lax.rsqrt(jnp.mean(o * o, axis=0, keepdims=True) + EPS)
            o_ref[:, hh * HEAD_DIM:(hh + 1) * HEAD_DIM] = (on.T * sg_ref[...] * (1.0 - lam_init)).astype(BF16)


def _diff_attn_prompt(qt, k, vt, slopes, lam_vecs, subln, lam_init, n_seq, seq, t):
    m = n_seq * seq
    nt = seq // t
    assert t <= POS_SPLIT * POS_SPLIT
    pairs = [(a, b) for a in range(nt) for b in range(a + 1)]
    qi = jnp.asarray([p[0] for p in pairs], jnp.int32)
    ki = jnp.asarray([p[1] for p in pairs], jnp.int32)
    top16 = lambda x: lax.bitcast_convert_type(
        lax.bitcast_convert_type(x, jnp.uint32) & jnp.uint32(0xFFFF0000), F32)
    s1 = top16(slopes)
    s2 = top16(slopes - s1)
    s3 = top16(slopes - s1 - s2)
    pieces = jnp.stack([s1, s2, s3])
    rows = jnp.concatenate([pieces * POS_SPLIT, pieces, jnp.zeros((AUG_ROWS - 6, N_HEADS), F32)]).T
    srows = jnp.broadcast_to(rows[:, :, None], (N_HEADS, AUG_ROWS, 2 * t)).astype(BF16)
    pos = jnp.arange(t)
    digits = jnp.stack([pos // POS_SPLIT] * 3 + [pos % POS_SPLIT] * 3, axis=1)
    kfeat = jnp.pad(digits, ((0, 0), (0, HEAD_DIM - 6))).astype(BF16)
    ones = jnp.zeros((AUG_ROWS, t), BF16).at[0].set(1)
    const = lambda shape: pl.BlockSpec(shape, lambda b, h, s, qi, ki: (0,) * len(shape))
    hps = HEADS_PER_STEP
    qspec = pl.BlockSpec((None, hps, HEAD_DIM, t), lambda b, h, s, qi, ki: (b, h, 0, qi[s]))
    kspec = pl.BlockSpec((None, hps, t, HEAD_DIM), lambda b, h, s, qi, ki: (b, h, ki[s], 0))
    vspec = pl.BlockSpec((None, hps, HEAD_DIM, t), lambda b, h, s, qi, ki: (b, h, 0, ki[s]))
    grid_spec = pltpu.PrefetchScalarGridSpec(
        num_scalar_prefetch=2,
        grid=(n_seq, N_HEADS // hps, len(pairs)),
        in_specs=[pl.BlockSpec(memory_space=pltpu.SMEM), qspec, kspec, vspec,
                  const((t, HEAD_DIM)),
                  pl.BlockSpec((hps, AUG_ROWS, 2 * t), lambda b, h, s, qi, ki: (h, 0, 0)),
                  const((AUG_ROWS, t)), const((4, QK_DIM)), const((1, HEAD_DIM))],
        out_specs=pl.BlockSpec((t, hps * HEAD_DIM), lambda b, h, s, qi, ki: (b * nt + qi[s], h)),
        scratch_shapes=[pltpu.VMEM((hps, 2 * HEAD_DIM, 2 * t), BF16), pltpu.VMEM((hps, 1, 2 * t), F32),
                        pltpu.VMEM((hps, HEAD_DIM + AUG_ROWS, 2 * t), F32)])
    return pl.pallas_call(
        functools.partial(_diff_attn_prompt_kernel, t=t, lam_init=lam_init),
        grid_spec=grid_spec,
        out_shape=jax.ShapeDtypeStruct((m, MIX_WIDTH), BF16),
        compiler_params=_cparams("parallel", "parallel", "arbitrary"),
        name="diff_attn_prompt",
    )(qi, ki, slopes, qt, k, vt, kfeat, srows, ones, lam_vecs, subln)


MAP_ROWS = 16


PAGES_PER_STEP = 8


def _diff_attn_sample_kernel(pt_ref, qm_ref, *refs, page, past, lam_init):
    k_refs = refs[:PAGES_PER_STEP]
    v_refs = refs[PAGES_PER_STEP:2 * PAGES_PER_STEP]
    bias_ref, slope_ref, kn_ref, vn_ref, lam_ref, sg_ref, o_ref, m_ref, l_ref, acc_ref = refs[2 * PAGES_PER_STEP:]
    p = pl.program_id(1)
    n_steps = pl.num_programs(1)

    @pl.when(p == 0)
    def _():
        m_ref[...] = jnp.full_like(m_ref, NEG_INF)
        l_ref[...] = jnp.zeros_like(l_ref)
        acc_ref[...] = jnp.zeros_like(acc_ref)

    qm = qm_ref[...]
    s = jnp.concatenate(
        [jnp.einsum("hmd,hkd->hmk", qm, k_ref[...].astype(BF16), preferred_element_type=F32) for k_ref in k_refs],
        axis=-1)
    s = s + bias_ref[...]
    off = slope_ref[...] * (p * (PAGES_PER_STEP * page) - past).astype(F32)
    m_old = m_ref[...]
    m_new = jnp.maximum(m_old, jnp.max(s, axis=-1, keepdims=True) + off)
    e = jnp.exp(s - (m_new - off))
    alpha = jnp.exp(m_old - m_new)
    l_ref[...] = alpha * l_ref[...] + jnp.sum(e, axis=-1, keepdims=True)
    e = e.astype(BF16)
    pv = None
    for j, v_ref in enumerate(v_refs):
        term = jnp.einsum("hmk,hkd->hmd", e[:, :, j * page:(j + 1) * page], v_ref[...].astype(BF16),
                          preferred_element_type=F32)
        pv = term if pv is None else pv + term
    acc_ref[...] = alpha * acc_ref[...] + pv
    m_ref[...] = m_new

    @pl.when(p == n_steps - 1)
    def _():
        kn = kn_ref[...].astype(BF16).astype(F32)
        vn = vn_ref[...].astype(BF16).astype(F32)
        s_new = jnp.sum(qm.astype(F32) * kn, axis=-1, keepdims=True)
        m_old = m_ref[...]
        m_fin = jnp.maximum(m_old, s_new)
        alpha = jnp.exp(m_old - m_fin)
        e_new = jnp.exp(s_new - m_fin)
        l = alpha * l_ref[...] + e_new
        o = (alpha * acc_ref[...] + e_new * vn) / l
        lam = _lambda_from(lam_ref, lam_init)
        d = o[:, 0:1, :] - lam * o[:, 1:2, :]
        o_ref[...] = _rms(d, sg_ref[...]) * (1.0 - lam_init)


def _diff_attn_sample(page_table, qm, cache_k, cache_v, layer, bias, slope_col, kn, vn, lam_vecs, subln, lam_init):
    n_seq, n_pages = page_table.shape
    page = cache_k.shape[3]
    past = n_pages * page
    pt = page_table.reshape(-1)
    per_seq = lambda rows: pl.BlockSpec((None, N_HEADS, rows, HEAD_DIM), lambda b, p, pt: (b, 0, 0, 0))
    kv = [pl.BlockSpec((None, None, N_HEADS, page, HEAD_DIM),
                       lambda b, p, pt, j=j: (layer, pt[b * n_pages + p * PAGES_PER_STEP + j], 0, 0, 0))
          for j in range(PAGES_PER_STEP)]
    const = lambda shape: pl.BlockSpec(shape, lambda b, p, pt: (0,) * len(shape))
    stat = (N_HEADS, MAP_ROWS, 1)
    grid_spec = pltpu.PrefetchScalarGridSpec(
        num_scalar_prefetch=1,
        grid=(n_seq, n_pages // PAGES_PER_STEP),
        in_specs=[per_seq(MAP_ROWS), *kv, *kv, const((N_HEADS, MAP_ROWS, PAGES_PER_STEP * page)), const(stat),
                  per_seq(1), per_seq(1), const((4, QK_DIM)), const((1, HEAD_DIM))],
        out_specs=per_seq(1),
        scratch_shapes=[pltpu.VMEM(stat, F32), pltpu.VMEM(stat, F32),
                        pltpu.VMEM((N_HEADS, MAP_ROWS, HEAD_DIM), F32)])
    return pl.pallas_call(
        functools.partial(_diff_attn_sample_kernel, page=page, past=past, lam_init=lam_init),
        grid_spec=grid_spec,
        out_shape=jax.ShapeDtypeStruct((n_seq, N_HEADS, 1, HEAD_DIM), F32),
        compiler_params=_cparams("parallel", "arbitrary"),
        name="diff_attn_sample",
    )(pt, qm, *([cache_k] * PAGES_PER_STEP), *([cache_v] * PAGES_PER_STEP), bias, slope_col, kn, vn, lam_vecs, subln)


def kernel(x_prompt, x_sample, mem_prompt, cache_mem_k, cache_mem_v, cache_k, cache_v, state_ffn_conv, page_table, norm_mix, norm_mem, norm_ffn, w_in_a, v_norm_a, spatial_w_a, spatial_b_a, w_out_a, w_in_b, q_norm_b, k_norm_b, lambda_q1, lambda_k1, lambda_q2, lambda_k2, subln_b, w_out_b, w_mem_kv, mem_q_norm, mem_k_norm, w_up, conv_w, conv_b, w_down):
    n_seq_p, seq, _ = x_prompt.shape
    n_seq_s = x_sample.shape[0]
    depth = norm_mix.shape[0]
    n_mem = mem_prompt.shape[1]
    m_p = n_seq_p * seq
    page = cache_k.shape[2]

    xp = x_prompt.reshape(m_p, D_MODEL)
    xs = jnp.pad(x_sample.reshape(n_seq_s, D_MODEL), ((0, SAMPLE_ROWS - n_seq_s), (0, 0)))
    mem = mem_prompt.reshape(n_seq_p * n_mem, D_MODEL)
    cmk = cache_mem_k.reshape(depth, n_seq_s, n_mem, MEM_WIDTH)
    cmv = cache_mem_v.reshape(depth, n_seq_s, n_mem, MEM_WIDTH)
    ck = cache_k.transpose(0, 1, 3, 2, 4)
    cv = cache_v.transpose(0, 1, 3, 2, 4)
    conv_state = jnp.pad(state_ffn_conv, ((0, 0), (0, SAMPLE_ROWS - n_seq_s), (0, 0), (0, 0)))

    slopes = 2.0 ** (-8.0 * jnp.arange(1, N_HEADS + 1, dtype=F32) / N_HEADS)
    slope_col = jnp.broadcast_to(slopes[:, None, None], (N_HEADS, MAP_ROWS, 1))
    step_pos = jnp.arange(PAGES_PER_STEP * page, dtype=F32)
    sample_bias = jnp.broadcast_to(slopes[:, None, None] * step_pos[None, None, :],
                                   (N_HEADS, MAP_ROWS, PAGES_PER_STEP * page))
    lane = jnp.arange(HEAD_DIM)
    map_mask = jnp.stack([lane < QK_DIM, lane >= QK_DIM]).astype(F32)
    head_view = lambda a: a[:n_seq_s].reshape(n_seq_s, N_HEADS, 1, HEAD_DIM)

    tril = jnp.tril(jnp.ones((CHUNK, CHUNK), bool))
    row2 = lambda v: v.reshape(1, -1)

    mem_k_p, mem_v_p, k_p, v_p, conv_p = [], [], [], [], []
    k_s, v_s, conv_s, chunk_v_s = [], [], [], []

    w_up_b, w_down_b, w_mem_b = w_up.astype(BF16), w_down.astype(BF16), w_mem_kv.astype(BF16)
    w_in_a_b, w_out_a_b = w_in_a.astype(BF16), w_out_a.astype(BF16)
    w_in_b_b, w_out_b_b = w_in_b.astype(BF16), w_out_b.astype(BF16)

    for i in range(depth):
        mk, mv = _in_proj(
            mem, row2(norm_mem[i]), w_mem_b, i,
            [dict(width=MEM_WIDTH, kind="gnorm", gain=0, f32="flat"),
             dict(width=MEM_WIDTH, kind="plain", f32="flat")],
            [mem_k_norm[i]], MEM_HEAD_DIM, tm=n_seq_p * n_mem, tn=MEM_WIDTH)
        mem_k_p.append(mk)
        mem_v_p.append(mv)

        if i % 2 == 0:
            la = i // 2
            w_in, w_out, wl = w_in_a_b, w_out_a_b, la
            regions = [dict(width=MIX_WIDTH, kind="gelu", f32="flat"),
                       dict(width=MIX_WIDTH, kind="gelu", f32="flat"),
                       dict(width=MEM_WIDTH, kind="plain", f32="flat")]
            wt = jnp.where(tril, spatial_w_a[la], 0.0).astype(BF16)
            bs = jnp.broadcast_to(spatial_b_a[la][:, :, None], (N_GROUPS, CHUNK, CHUNK))

            u, v, qm_p = _in_proj(xp, row2(norm_mix[i]), w_in, wl, regions, [], QK_DIM, tm=1024, tn=512, x_buffers=1)
            mix_p = _mixer_a(u, v, row2(v_norm_a[la]), wt, bs, tm=512)

            u, v, qm_s = _in_proj(xs, row2(norm_mix[i]), w_in, wl, regions, [], QK_DIM, tm=SAMPLE_ROWS, tn=512)
            w0 = jnp.repeat(wt[:, 0, 0].astype(F32), CHUNK).reshape(1, MIX_WIDTH)
            b0 = jnp.repeat(spatial_b_a[la][:, 0], CHUNK).reshape(1, MIX_WIDTH)
            mix_s, vrow = _mixer_a_sample(u, v, row2(v_norm_a[la]), w0, b0)
            chunk_v_s.append(vrow[:n_seq_s])
        else:
            lb = i // 2
            lam_init = 0.8 - 0.6 * math.exp(-0.3 * i)
            w_in, w_out, wl = w_in_b_b, w_out_b_b, lb
            lam_vecs = jnp.stack([lambda_q1[lb], lambda_k1[lb], lambda_q2[lb], lambda_k2[lb]])
            subln = row2(subln_b[lb])
            gains = [q_norm_b[lb], k_norm_b[lb]]
            qk_scale = QK_DIM ** -0.5

            regions = [dict(width=MIX_WIDTH, kind="gnorm", gain=0, bf16="cols", scale=qk_scale),
                       dict(width=MIX_WIDTH, kind="gnorm", gain=1, f32="rows", bf16="rows"),
                       dict(width=MIX_WIDTH, kind="plain", f32="rows", bf16="cols"),
                       dict(width=MEM_WIDTH, kind="plain", f32="flat")]
            qt, kf, kb, vf, vt, qm_p = _in_proj(xp, row2(norm_mix[i]), w_in, wl, regions, gains, QK_DIM, tm=1024,
                                                tn=512, seq=seq, x_buffers=1)
            k_p.append(kf)
            v_p.append(vf)
            mix_p = _diff_attn_prompt(qt, kb, vt, slopes, lam_vecs, subln, lam_init, n_seq_p, seq, t=1024)

            regions = [dict(width=MIX_WIDTH, kind="gnorm", gain=0, f32="flat"),
                       dict(width=MIX_WIDTH, kind="gnorm", gain=1, f32="flat"),
                       dict(width=MIX_WIDTH, kind="plain", f32="flat"),
                       dict(width=MEM_WIDTH, kind="plain", f32="flat")]
            q, kf, vf, qm_s = _in_proj(xs, row2(norm_mix[i]), w_in, wl, regions, gains, QK_DIM, tm=SAMPLE_ROWS,
                                       tn=512)
            k_s.append(kf[:n_seq_s])
            v_s.append(vf[:n_seq_s])
            qm = jnp.pad(head_view(q * qk_scale) * map_mask[None, None],
                         ((0, 0), (0, 0), (0, MAP_ROWS - 2), (0, 0))).astype(BF16)
            o = _diff_attn_sample(page_table, qm, ck, cv, lb, sample_bias, slope_col, head_view(kf), head_view(vf),
                                  lam_vecs, subln, lam_init)
            mix_s = jnp.pad(o.reshape(n_seq_s, MIX_WIDTH), ((0, SAMPLE_ROWS - n_seq_s), (0, 0))).astype(BF16)

        mo_p = _mem_attend(qm_p, row2(mem_q_norm[i]), mk, mv, rows_per_batch=seq, tm=512)
        mo_s = _mem_attend_sample(qm_s, row2(mem_q_norm[i]), cmk, cmv, i, n_seq_s)
        xp = _out_proj(mix_p, mo_p, w_out, wl, xp, tm=1024, tn=1024)
        xs = _out_proj(mix_s, mo_s, w_out, wl, xs, tm=SAMPLE_ROWS, tn=1024)

        xp, cp = _ffn_prompt(xp, row2(norm_ffn[i]), w_up_b, conv_w[i], row2(conv_b[i]), w_down_b, i, seq,
                             tm=1024, tf=512)
        xs, a_s = _ffn_sample(xs, row2(norm_ffn[i]), w_up_b, conv_w[i], row2(conv_b[i]),
                              conv_state[i, :, 0], conv_state[i, :, 1], w_down_b, i, tf=512)
        conv_p.append(cp)
        conv_s.append(jnp.stack([state_ffn_conv[i, :, 1], a_s[:n_seq_s]], axis=1))

    heads = lambda a, b, t: a.reshape(b, t, N_HEADS, HEAD_DIM)
    return (xp.reshape(n_seq_p, seq, D_MODEL),
            xs[:n_seq_s].reshape(n_seq_s, 1, D_MODEL),
            jnp.stack(mem_k_p).reshape(depth, n_seq_p, n_mem, MEM_HEADS, MEM_HEAD_DIM),
            jnp.stack(mem_v_p).reshape(depth, n_seq_p, n_mem, MEM_HEADS, MEM_HEAD_DIM),
            jnp.stack(k_p).transpose(0, 1, 3, 2, 4),
            jnp.stack(v_p).transpose(0, 1, 3, 2, 4),
            jnp.stack(conv_p),
            jnp.stack([heads(a, n_seq_s, 1) for a in k_s]),
            jnp.stack([heads(a, n_seq_s, 1) for a in v_s]),
            jnp.stack(conv_s),
            jnp.stack(chunk_v_s).reshape(len(chunk_v_s), n_seq_s, 1, MIX_WIDTH))
```

```python
import functools
import math

import jax
import jax.numpy as jnp
from jax import lax
from jax.experimental import pallas as pl
from jax.experimental.pallas import tpu as pltpu

F32 = jnp.float32
BF16 = jnp.bfloat16

D_MODEL = 2048
MEM_HEADS = 4
MEM_HEAD_DIM = 128
MEM_WIDTH = MEM_HEADS * MEM_HEAD_DIM
MIX_WIDTH = D_MODEL - MEM_WIDTH
CHUNK = 128
N_GROUPS = 12
N_HEADS = 12
HEAD_DIM = 128
QK_DIM = 64
D_FF = 5632
CONV_W = 3
EPS = 1e-6
NEG_INF = -1e30
SAMPLE_ROWS = 16
MXU_WIDTH = 256

VMEM_LIMIT = 56 * 1024 * 1024


def _cparams(*sem):
    return pltpu.CompilerParams(dimension_semantics=sem, vmem_limit_bytes=VMEM_LIMIT)


def _rms(x, g):
    return x * lax.rsqrt(jnp.mean(x * x, axis=-1, keepdims=True) + EPS) * g


def _dot(a, b):
    return jnp.dot(a, b, preferred_element_type=F32)


def _dot_nt(a, b):
    return lax.dot_general(a, b, (((1,), (1,)), ((), ())), preferred_element_type=F32)


def _block_diag_ones(n, group):
    r = jnp.arange(n) // group
    return (r[:, None] == r[None, :]).astype(BF16)


def _in_proj_kernel(x_ref, g_ref, w_ref, bd_ref, *refs, regions, n_gain, group, emit_w):
    gain_refs = refs[:n_gain]
    n_out = sum((r["f32"] is not None) + (r["bf16"] is not None) for r in regions)
    out_refs = refs[n_gain:n_gain + n_out]
    wcopy_ref = refs[n_gain + n_out] if emit_w else None
    h_ref = refs[-1]
    j = pl.program_id(1)

    @pl.when(j == 0)
    def _():
        h_ref[...] = _rms(x_ref[...], g_ref[...]).astype(BF16)

    def weight():
        w = w_ref[...].astype(BF16)
        if emit_w:
            wcopy_ref[...] = w
        return w

    def store(ref, val, layout):
        if layout == "flat":
            ref[...] = val.astype(ref.dtype)
            return
        for hh in range(val.shape[1] // HEAD_DIM):
            head = val[:, hh * HEAD_DIM:(hh + 1) * HEAD_DIM]
            ref[hh] = head.T.astype(ref.dtype) if layout == "cols" else head.astype(ref.dtype)

    for r in regions:
        @pl.when((j >= r["start"]) & (j < r["start"] + r["ntiles"]))
        def _(r=r):
            z = _dot(h_ref[...], weight())
            if r["kind"] == "gelu":
                y = jax.nn.gelu(z)
            elif r["kind"] == "plain":
                y = z
            else:
                z2 = (z * z).astype(BF16)
                bw = bd_ref.shape[0]
                ss = jnp.concatenate([_dot(z2[:, c:c + bw], bd_ref[...]) for c in range(0, z2.shape[1], bw)], axis=1)
                y = z * lax.rsqrt(ss * (1.0 / group) + EPS) * gain_refs[r["gain"]][...]
            if r["f32"] is not None:
                store(out_refs[r["f32"]], y, r["f32_layout"])
            if r["bf16"] is not None:
                store(out_refs[r["bf16"]], y if r["scale"] == 1.0 else y * r["scale"], r["bf16_layout"])


def _in_proj(x, g, w, layer, regions, gains, group, tm, tn, seq=None, x_buffers=2, emit_w=False):
    m, d = x.shape
    assert not emit_w or m == tm
    regs, out_shapes, out_specs = [], [], []
    start = 0
    for r in regions:
        nt = r["width"] // tn
        reg = dict(start=start, ntiles=nt, kind=r["kind"], gain=r.get("gain"), scale=r.get("scale", 1.0),
                   f32=None, bf16=None)
        for key, dt in (("f32", F32), ("bf16", BF16)):
            layout = r.get(key)
            reg[key + "_layout"] = layout
            if layout is None:
                continue
            reg[key] = len(out_shapes)
            tile = lambda i, j, s=start, n=nt: jnp.clip(j - s, 0, n - 1)
            if layout == "flat":
                out_shapes.append(jax.ShapeDtypeStruct((m, r["width"]), dt))
                out_specs.append(pl.BlockSpec((tm, tn), lambda i, j, tile=tile: (i, tile(i, j))))
                continue
            tps = seq // tm
            heads, hpt = r["width"] // HEAD_DIM, tn // HEAD_DIM
            if layout == "rows":
                out_shapes.append(jax.ShapeDtypeStruct((m // seq, heads, seq, HEAD_DIM), dt))
                out_specs.append(pl.BlockSpec(
                    (None, hpt, tm, HEAD_DIM), lambda i, j, tile=tile, tps=tps: (i // tps, tile(i, j), i % tps, 0)))
            else:
                out_shapes.append(jax.ShapeDtypeStruct((m // seq, heads, HEAD_DIM, seq), dt))
                out_specs.append(pl.BlockSpec(
                    (None, hpt, HEAD_DIM, tm), lambda i, j, tile=tile, tps=tps: (i // tps, tile(i, j), 0, i % tps)))
        regs.append(reg)
        start += nt
    assert start * tn == w.shape[-1]
    if emit_w:
        out_shapes.append(jax.ShapeDtypeStruct((d, w.shape[-1]), BF16))
        out_specs.append(pl.BlockSpec((d, tn), lambda i, j: (0, j)))
    if w.ndim == 2:
        w_spec = pl.BlockSpec((d, tn), lambda i, j: (0, j))
    else:
        w_spec = pl.BlockSpec((None, d, tn), lambda i, j: (layer, 0, j))
    gain_tiles = [jnp.tile(gv.reshape(1, -1), (1, tn // gv.size)) for gv in gains]
    in_specs = [
        pl.BlockSpec((tm, d), lambda i, j: (i, 0), pipeline_mode=pl.Buffered(x_buffers)),
        pl.BlockSpec((1, d), lambda i, j: (0, 0)),
        w_spec,
        pl.BlockSpec((MXU_WIDTH, MXU_WIDTH), lambda i, j: (0, 0)),
    ] + [pl.BlockSpec((1, tn), lambda i, j: (0, 0)) for _ in gain_tiles]
    return pl.pallas_call(
        functools.partial(_in_proj_kernel, regions=regs, n_gain=len(gain_tiles), group=group, emit_w=emit_w),
        grid=(m // tm, start),
        in_specs=in_specs,
        out_specs=out_specs,
        out_shape=out_shapes,
        scratch_shapes=[pltpu.VMEM((tm, d), BF16)],
        compiler_params=_cparams("parallel", "arbitrary"),
        name="in_proj",
    )(x, g, w, _block_diag_ones(MXU_WIDTH, group), *gain_tiles)


def _mixer_a_kernel(u_ref, v_ref, vg_ref, wt_ref, bs_ref, o_ref, *, n_chunks):
    vb = _rms(v_ref[...], vg_ref[...]).astype(BF16)
    for g in range(N_GROUPS):
        cols = slice(g * CHUNK, (g + 1) * CHUNK)
        rhs = jnp.concatenate([vb[c * CHUNK:(c + 1) * CHUNK, cols] for c in range(n_chunks)], axis=1)
        mixed = _dot(wt_ref[g], rhs)
        for c in range(n_chunks):
            rows = slice(c * CHUNK, (c + 1) * CHUNK)
            mc = mixed[:, c * CHUNK:(c + 1) * CHUNK] + bs_ref[g]
            o_ref[rows, cols] = (u_ref[rows, cols] * mc).astype(BF16)


def _mixer_a(u, v, vgain, wt, bs, tm):
    m = u.shape[0]
    row = pl.BlockSpec((tm, MIX_WIDTH), lambda i: (i, 0))
    return pl.pallas_call(
        functools.partial(_mixer_a_kernel, n_chunks=tm // CHUNK),
        grid=(m // tm,),
        in_specs=[row, row,
                  pl.BlockSpec((1, MIX_WIDTH), lambda i: (0, 0)),
                  pl.BlockSpec((N_GROUPS, CHUNK, CHUNK), lambda i: (0, 0, 0)),
                  pl.BlockSpec((N_GROUPS, CHUNK, CHUNK), lambda i: (0, 0, 0))],
        out_specs=row,
        out_shape=jax.ShapeDtypeStruct((m, MIX_WIDTH), BF16),
        compiler_params=_cparams("parallel"),
        name="mixer_a",
    )(u, v, vgain, wt, bs)


def _mixer_a_sample_kernel(u_ref, v_ref, vg_ref, w0_ref, b0_ref, o_ref, vn_ref):
    vn = _rms(v_ref[...], vg_ref[...])
    vn_ref[...] = vn
    mixed = vn.astype(BF16).astype(F32) * w0_ref[...] + b0_ref[...]
    o_ref[...] = (u_ref[...] * mixed).astype(BF16)


def _mixer_a_sample(u, v, vgain, w0, b0):
    m = u.shape[0]
    full = pl.BlockSpec((m, MIX_WIDTH), lambda: (0, 0))
    vec = pl.BlockSpec((1, MIX_WIDTH), lambda: (0, 0))
    return pl.pallas_call(
        _mixer_a_sample_kernel,
        in_specs=[full, full, vec, vec, vec],
        out_specs=[full, full],
        out_shape=[jax.ShapeDtypeStruct((m, MIX_WIDTH), BF16), jax.ShapeDtypeStruct((m, MIX_WIDTH), F32)],
        name="mixer_a_sample",
    )(u, v, vgain, w0, b0)


def _mem_attend_tile(q, qg, k, v):
    outs = []
    for h in range(MEM_HEADS):
        cols = slice(h * MEM_HEAD_DIM, (h + 1) * MEM_HEAD_DIM)
        qn = _rms(q[:, cols], qg).astype(BF16)
        s = _dot_nt(qn, k[:, cols].astype(BF16)) * (MEM_HEAD_DIM ** -0.5)
        s = s - jnp.max(s, axis=-1, keepdims=True)
        e = jnp.exp(s)
        p = e / jnp.sum(e, axis=-1, keepdims=True)
        outs.append(_dot(p.astype(BF16), v[:, cols].astype(BF16)))
    return jnp.concatenate(outs, axis=1)


def _mem_attend_kernel(q_ref, qg_ref, k_ref, v_ref, o_ref):
    o_ref[...] = _mem_attend_tile(q_ref[...], qg_ref[...], k_ref[...], v_ref[...]).astype(BF16)


def _mem_attend(q, qg, k, v, rows_per_batch, tm):
    m = q.shape[0]
    n_mem = k.shape[0] // (m // rows_per_batch)
    tiles_per_batch = rows_per_batch // tm
    kv = pl.BlockSpec((n_mem, MEM_WIDTH), lambda i: (i // tiles_per_batch, 0))
    return pl.pallas_call(
        _mem_attend_kernel,
        grid=(m // tm,),
        in_specs=[pl.BlockSpec((tm, MEM_WIDTH), lambda i: (i, 0)),
                  pl.BlockSpec((1, MEM_HEAD_DIM), lambda i: (0, 0)), kv, kv],
        out_specs=pl.BlockSpec((tm, MEM_WIDTH), lambda i: (i, 0)),
        out_shape=jax.ShapeDtypeStruct((m, MEM_WIDTH), BF16),
        compiler_params=_cparams("parallel"),
        name="mem_attend",
    )(q, qg, k, v)


def _mem_attend_sample_kernel(q_ref, qg_ref, k_ref, v_ref, o_ref):
    b = pl.program_id(0)
    o = _mem_attend_tile(q_ref[...], qg_ref[...], k_ref[...], v_ref[...]).astype(BF16)

    @pl.when(b == 0)
    def _():
        o_ref[...] = jnp.zeros_like(o_ref)

    row = lax.broadcasted_iota(jnp.int32, o.shape, 0)
    o_ref[...] = jnp.where(row == b, o, o_ref[...])


def _mem_attend_sample(q, qg, k, v, layer, n_seq):
    m = q.shape[0]
    n_mem = k.shape[2]
    kv = pl.BlockSpec((None, None, n_mem, MEM_WIDTH), lambda b: (layer, b, 0, 0))
    full = pl.BlockSpec((m, MEM_WIDTH), lambda b: (0, 0))
    return pl.pallas_call(
        _mem_attend_sample_kernel,
        grid=(n_seq,),
        in_specs=[full, pl.BlockSpec((1, MEM_HEAD_DIM), lambda b: (0, 0)), kv, kv],
        out_specs=full,
        out_shape=jax.ShapeDtypeStruct((m, MEM_WIDTH), BF16),
        compiler_params=_cparams("arbitrary"),
        name="mem_attend_sample",
    )(q, qg, k, v)


def _out_proj_kernel(a_ref, b_ref, wa_ref, wb_ref, x_ref, o_ref):
    o_ref[...] = x_ref[...] + _dot(a_ref[...], wa_ref[...]) + _dot(b_ref[...], wb_ref[...])


def _out_proj(mix, mo, w, x, tm, tn):
    m = x.shape[0]
    assert MIX_WIDTH % MEM_WIDTH == 0
    return pl.pallas_call(
        _out_proj_kernel,
        grid=(m // tm, D_MODEL // tn),
        in_specs=[pl.BlockSpec((tm, MIX_WIDTH), lambda i, j: (i, 0)),
                  pl.BlockSpec((tm, MEM_WIDTH), lambda i, j: (i, 0)),
                  pl.BlockSpec((MIX_WIDTH, tn), lambda i, j: (0, j)),
                  pl.BlockSpec((MEM_WIDTH, tn), lambda i, j: (MIX_WIDTH // MEM_WIDTH, j)),
                  pl.BlockSpec((tm, tn), lambda i, j: (i, j))],
        out_specs=pl.BlockSpec((tm, tn), lambda i, j: (i, j)),
        out_shape=jax.ShapeDtypeStruct((m, D_MODEL), F32),
        compiler_params=_cparams("parallel", "arbitrary"),
        name="out_proj",
    )(mix, mo, w, w, x)


def _out_proj_sample_kernel(a_ref, b_ref, w_ref, x_ref, o_ref, wcopy_ref):
    w = w_ref[...].astype(BF16)
    wcopy_ref[...] = w
    o_ref[...] = x_ref[...] + _dot(a_ref[...], w[:MIX_WIDTH]) + _dot(b_ref[...], w[MIX_WIDTH:])


def _out_proj_sample(mix, mo, w, layer, x, tn):
    m = x.shape[0]
    return pl.pallas_call(
        _out_proj_sample_kernel,
        grid=(D_MODEL // tn,),
        in_specs=[pl.BlockSpec((m, MIX_WIDTH), lambda j: (0, 0)),
                  pl.BlockSpec((m, MEM_WIDTH), lambda j: (0, 0)),
                  pl.BlockSpec((None, D_MODEL, tn), lambda j: (layer, 0, j)),
                  pl.BlockSpec((m, tn), lambda j: (0, j))],
        out_specs=[pl.BlockSpec((m, tn), lambda j: (0, j)),
                   pl.BlockSpec((D_MODEL, tn), lambda j: (0, j))],
        out_shape=[jax.ShapeDtypeStruct((m, D_MODEL), F32), jax.ShapeDtypeStruct((D_MODEL, D_MODEL), BF16)],
        compiler_params=_cparams("arbitrary"),
        name="out_proj_sample",
    )(mix, mo, w, x)


HALO = 16


def _ffn_prompt_kernel(x_ref, halo_ref, g_ref, wa_ref, wb_ref, cw_ref, cb_ref, wd_ref, o_ref, conv_ref,
                       h_ref, a_ref, *, tm, tiles_per_seq):
    i = pl.program_id(0)
    f = pl.program_id(1)

    @pl.when(f == 0)
    def _():
        h_ref[HALO:, :] = _rms(x_ref[...], g_ref[...]).astype(BF16)
        hh = _rms(halo_ref[...], g_ref[...])
        h_ref[:HALO, :] = jnp.where(i % tiles_per_seq == 0, 0.0, hh).astype(BF16)
        o_ref[...] = x_ref[...]

    a_ref[...] = _dot(h_ref[...], wa_ref[...])
    b = _dot(h_ref[HALO:, :], wb_ref[...])
    c = cb_ref[...]
    for tap in range(CONV_W):
        off = HALO - (CONV_W - 1) + tap
        c = c + a_ref[off:off + tm, :] * cw_ref[tap:tap + 1, :]
    act = (jax.nn.silu(c) * b).astype(BF16)
    half = D_MODEL // 2
    o_ref[:, :half] += _dot(act, wd_ref[:, :half])
    o_ref[:, half:] += _dot(act, wd_ref[:, half:])
    conv_ref[...] = a_ref[tm + HALO - 8:tm + HALO, :]


def _ffn_prompt(x, g, w_gate, w_lin, cw, cb, w_down, seq, tm, tf):
    m = x.shape[0]
    nf = D_FF // tf
    tiles_per_seq = seq // tm
    halo_blocks = tm // HALO
    once = pl.Buffered(1)
    out, conv = pl.pallas_call(
        functools.partial(_ffn_prompt_kernel, tm=tm, tiles_per_seq=tiles_per_seq),
        grid=(m // tm, nf),
        in_specs=[pl.BlockSpec((tm, D_MODEL), lambda i, f: (i, 0)),
                  pl.BlockSpec((HALO, D_MODEL), lambda i, f: (jnp.maximum(i * halo_blocks - 1, 0), 0)),
                  pl.BlockSpec((1, D_MODEL), lambda i, f: (0, 0)),
                  pl.BlockSpec((D_MODEL, tf), lambda i, f: (0, f)),
                  pl.BlockSpec((D_MODEL, tf), lambda i, f: (0, f)),
                  pl.BlockSpec((CONV_W, tf), lambda i, f: (0, f)),
                  pl.BlockSpec((1, tf), lambda i, f: (0, f)),
                  pl.BlockSpec((tf, D_MODEL), lambda i, f: (f, 0))],
        out_specs=[pl.BlockSpec((tm, D_MODEL), lambda i, f: (i, 0), pipeline_mode=once),
                   pl.BlockSpec((None, 8, tf), lambda i, f: (i, 0, f))],
        out_shape=[jax.ShapeDtypeStruct((m, D_MODEL), F32),
                   jax.ShapeDtypeStruct((m // tm, 8, D_FF), F32)],
        scratch_shapes=[pltpu.VMEM((tm + HALO, D_MODEL), BF16), pltpu.VMEM((tm + HALO, tf), F32)],
        compiler_params=_cparams("parallel", "arbitrary"),
        name="ffn_prompt",
    )(x, x, g, w_gate, w_lin, cw, cb, w_down)
    return out, conv[tiles_per_seq - 1::tiles_per_seq, 8 - (CONV_W - 1):, :]


def _ffn_sample_kernel(x_ref, g_ref, wa_ref, wb_ref, cw_ref, cb_ref, p0_ref, p1_ref, wd_ref, o_ref, a_out_ref,
                       wa_copy_ref, wb_copy_ref, wd_copy_ref, h_ref):
    f = pl.program_id(0)

    @pl.when(f == 0)
    def _():
        h_ref[...] = _rms(x_ref[...], g_ref[...]).astype(BF16)

    wa = wa_ref[...].astype(BF16)
    wb = wb_ref[...].astype(BF16)
    wd = wd_ref[...].astype(BF16)
    wa_copy_ref[...] = wa
    wb_copy_ref[...] = wb
    wd_copy_ref[...] = wd
    a = _dot(h_ref[...], wa)
    b = _dot(h_ref[...], wb)
    a_out_ref[...] = a
    c = cb_ref[...] + p0_ref[...] * cw_ref[0:1, :] + p1_ref[...] * cw_ref[1:2, :] + a * cw_ref[2:3, :]
    y = _dot((jax.nn.silu(c) * b).astype(BF16), wd)

    @pl.when(f == 0)
    def _():
        o_ref[...] = x_ref[...] + y

    @pl.when(f > 0)
    def _():
        o_ref[...] += y


def _ffn_sample(x, g, w_up, cw, cb, p0, p1, w_down, layer, tf):
    m = x.shape[0]
    nf = D_FF // tf
    up_tile = pl.BlockSpec((D_MODEL, tf), lambda f: (0, f))
    down_tile = pl.BlockSpec((tf, D_MODEL), lambda f: (f, 0))
    full = pl.BlockSpec((m, D_MODEL), lambda f: (0, 0))
    col = pl.BlockSpec((m, tf), lambda f: (0, f))
    return pl.pallas_call(
        _ffn_sample_kernel,
        grid=(nf,),
        in_specs=[full,
                  pl.BlockSpec((1, D_MODEL), lambda f: (0, 0)),
                  pl.BlockSpec((None, D_MODEL, tf), lambda f: (layer, 0, f)),
                  pl.BlockSpec((None, D_MODEL, tf), lambda f: (layer, 0, nf + f)),
                  pl.BlockSpec((CONV_W, tf), lambda f: (0, f)),
                  pl.BlockSpec((1, tf), lambda f: (0, f)),
                  col, col,
                  pl.BlockSpec((None, tf, D_MODEL), lambda f: (layer, f, 0))],
        out_specs=[full, col, up_tile, up_tile, down_tile],
        out_shape=[jax.ShapeDtypeStruct((m, D_MODEL), F32), jax.ShapeDtypeStruct((m, D_FF), F32),
                   jax.ShapeDtypeStruct((D_MODEL, D_FF), BF16), jax.ShapeDtypeStruct((D_MODEL, D_FF), BF16),
                   jax.ShapeDtypeStruct((D_FF, D_MODEL), BF16)],
        scratch_shapes=[pltpu.VMEM((m, D_MODEL), BF16)],
        compiler_params=_cparams("arbitrary"),
        name="ffn_sample",
    )(x, g, w_up, w_up, cw, cb, p0, p1, w_down)


def _lambda_from(lam_ref, lam_init):
    v = lam_ref[...]
    a = jnp.sum(v[0:1] * v[1:2], axis=-1, keepdims=True)
    b = jnp.sum(v[2:3] * v[3:4], axis=-1, keepdims=True)
    return jnp.exp(a) - jnp.exp(b) + lam_init


HEADS_PER_STEP = 4
AUG_ROWS = 16
POS_SPLIT = 32


def _diff_attn_prompt_kernel(qi_ref, ki_ref, slope_ref, qt_ref, k_ref, vt_ref, kfeat_ref, srows_ref, ones_ref,
                             lam_ref, sg_ref, o_ref, qs_ref, m_ref, acc_ref, *, t, lam_init):
    hp = pl.program_id(1)
    step = pl.program_id(2)
    qi = qi_ref[step]
    ki = ki_ref[step]
    heads = range(HEADS_PER_STEP)

    @pl.when(ki == 0)
    def _():
        for hh in heads:
            qt = qt_ref[hh]
            dim = lax.broadcasted_iota(jnp.int32, qt.shape, 0)
            zero = jnp.zeros_like(qt)
            qs_ref[hh, :HEAD_DIM, :t] = jnp.where(dim < QK_DIM, qt, zero)
            qs_ref[hh, :HEAD_DIM, t:] = jnp.where(dim >= QK_DIM, qt, zero)
            qs_ref[hh, HEAD_DIM:HEAD_DIM + AUG_ROWS, :] = srows_ref[hh]
            qs_ref[hh, HEAD_DIM + AUG_ROWS:, :] = jnp.zeros((HEAD_DIM - AUG_ROWS, 2 * t), BF16)
        m_ref[...] = jnp.full_like(m_ref, NEG_INF)
        acc_ref[...] = jnp.zeros_like(acc_ref)

    def update(masked):
        for hh in heads:
            slope = slope_ref[hp * HEADS_PER_STEP + hh]
            s = _dot(jnp.concatenate([k_ref[hh], kfeat_ref[...]], axis=1), qs_ref[hh])
            if masked:
                key = lax.broadcasted_iota(jnp.int32, (t, t), 0)
                qry = lax.broadcasted_iota(jnp.int32, (t, t), 1)
                keep = key <= qry
                s = jnp.concatenate([jnp.where(keep, s[:, :t], NEG_INF), jnp.where(keep, s[:, t:], NEG_INF)],
                                    axis=1)
            off = slope * ((ki - qi) * t).astype(F32)
            m_old = m_ref[hh]
            m_new = jnp.maximum(m_old, jnp.max(s, axis=0, keepdims=True) + off)
            p = jnp.exp(s - (m_new - off)).astype(BF16)
            alpha = jnp.exp(m_old - m_new)
            vt_aug = jnp.concatenate([vt_ref[hh], ones_ref[...]], axis=0)
            acc_ref[hh] = alpha * acc_ref[hh] + _dot(vt_aug, p)
            m_ref[hh] = m_new

    @pl.when(ki < qi)
    def _():
        update(False)

    @pl.when(ki == qi)
    def _():
        update(True)
        lam = _lambda_from(lam_ref, lam_init)
        for hh in heads:
            acc = acc_ref[hh]
            inv = 1.0 / acc[HEAD_DIM:HEAD_DIM + 1, :]
            o = acc[:HEAD_DIM, :t] * inv[:, :t] - lam * (acc[:HEAD_DIM, t:] * inv[:, t:])
            on = o * lax.rsqrt(jnp.mean(o * o, axis=0, keepdims=True) + EPS)
            o_ref[:, hh * HEAD_DIM:(hh + 1) * HEAD_DIM] = (on.T * sg_ref[...] * (1.0 - lam_init)).astype(BF16)


def _diff_attn_prompt(qt, k, vt, slopes, lam_vecs, subln, lam_init, n_seq, seq, t):
    m = n_seq * seq
    nt = seq // t
    assert t <= POS_SPLIT * POS_SPLIT
    pairs = [(a, b) for a in range(nt) for b in range(a + 1)]
    qi = jnp.asarray([p[0] for p in pairs], jnp.int32)
    ki = jnp.asarray([p[1] for p in pairs], jnp.int32)
    top16 = lambda x: lax.bitcast_convert_type(
        lax.bitcast_convert_type(x, jnp.uint32) & jnp.uint32(0xFFFF0000), F32)
    s1 = top16(slopes)
    s2 = top16(slopes - s1)
    s3 = top16(slopes - s1 - s2)
    pieces = jnp.stack([s1, s2, s3])
    rows = jnp.concatenate([pieces * POS_SPLIT, pieces, jnp.zeros((AUG_ROWS - 6, N_HEADS), F32)]).T
    srows = jnp.broadcast_to(rows[:, :, None], (N_HEADS, AUG_ROWS, 2 * t)).astype(BF16)
    pos = jnp.arange(t)
    digits = jnp.stack([pos // POS_SPLIT] * 3 + [pos % POS_SPLIT] * 3, axis=1)
    kfeat = jnp.pad(digits, ((0, 0), (0, HEAD_DIM - 6))).astype(BF16)
    ones = jnp.zeros((AUG_ROWS, t), BF16).at[0].set(1)
    const = lambda shape: pl.BlockSpec(shape, lambda b, h, s, qi, ki: (0,) * len(shape))
    hps = HEADS_PER_STEP
    qspec = pl.BlockSpec((None, hps, HEAD_DIM, t), lambda b, h, s, qi, ki: (b, h, 0, qi[s]))
    kspec = pl.BlockSpec((None, hps, t, HEAD_DIM), lambda b, h, s, qi, ki: (b, h, ki[s], 0))
    vspec = pl.BlockSpec((None, hps, HEAD_DIM, t), lambda b, h, s, qi, ki: (b, h, 0, ki[s]))
    grid_spec = pltpu.PrefetchScalarGridSpec(
        num_scalar_prefetch=2,
        grid=(n_seq, N_HEADS // hps, len(pairs)),
        in_specs=[pl.BlockSpec(memory_space=pltpu.SMEM), qspec, kspec, vspec,
                  const((t, HEAD_DIM)),
                  pl.BlockSpec((hps, AUG_ROWS, 2 * t), lambda b, h, s, qi, ki: (h, 0, 0)),
                  const((AUG_ROWS, t)), const((4, QK_DIM)), const((1, HEAD_DIM))],
        out_specs=pl.BlockSpec((t, hps * HEAD_DIM), lambda b, h, s, qi, ki: (b * nt + qi[s], h)),
        scratch_shapes=[pltpu.VMEM((hps, 2 * HEAD_DIM, 2 * t), BF16), pltpu.VMEM((hps, 1, 2 * t), F32),
                        pltpu.VMEM((hps, HEAD_DIM + AUG_ROWS, 2 * t), F32)])
    return pl.pallas_call(
        functools.partial(_diff_attn_prompt_kernel, t=t, lam_init=lam_init),
        grid_spec=grid_spec,
        out_shape=jax.ShapeDtypeStruct((m, MIX_WIDTH), BF16),
        compiler_params=_cparams("parallel", "parallel", "arbitrary"),
        name="diff_attn_prompt",
    )(qi, ki, slopes, qt, k, vt, kfeat, srows, ones, lam_vecs, subln)


MAP_ROWS = 16


PAGES_PER_STEP = 8


def _diff_attn_sample_kernel(pt_ref, qm_ref, *refs, page, past, lam_init):
    k_refs = refs[:PAGES_PER_STEP]
    v_refs = refs[PAGES_PER_STEP:2 * PAGES_PER_STEP]
    bias_ref, slope_ref, kn_ref, vn_ref, lam_ref, sg_ref, o_ref, m_ref, l_ref, acc_ref = refs[2 * PAGES_PER_STEP:]
    p = pl.program_id(1)
    n_steps = pl.num_programs(1)

    @pl.when(p == 0)
    def _():
        m_ref[...] = jnp.full_like(m_ref, NEG_INF)
        l_ref[...] = jnp.zeros_like(l_ref)
        acc_ref[...] = jnp.zeros_like(acc_ref)

    qm = qm_ref[...]
    s = jnp.concatenate(
        [jnp.einsum("hmd,hkd->hmk", qm, k_ref[...].astype(BF16), preferred_element_type=F32) for k_ref in k_refs],
        axis=-1)
    s = s + bias_ref[...]
    off = slope_ref[...] * (p * (PAGES_PER_STEP * page) - past).astype(F32)
    m_old = m_ref[...]
    m_new = jnp.maximum(m_old, jnp.max(s, axis=-1, keepdims=True) + off)
    e = jnp.exp(s - (m_new - off))
    alpha = jnp.exp(m_old - m_new)
    l_ref[...] = alpha * l_ref[...] + jnp.sum(e, axis=-1, keepdims=True)
    e = e.astype(BF16)
    pv = None
    for j, v_ref in enumerate(v_refs):
        term = jnp.einsum("hmk,hkd->hmd", e[:, :, j * page:(j + 1) * page], v_ref[...].astype(BF16),
                          preferred_element_type=F32)
        pv = term if pv is None else pv + term
    acc_ref[...] = alpha * acc_ref[...] + pv
    m_ref[...] = m_new

    @pl.when(p == n_steps - 1)
    def _():
        kn = kn_ref[...].astype(BF16).astype(F32)
        vn = vn_ref[...].astype(BF16).astype(F32)
        s_new = jnp.sum(qm.astype(F32) * kn, axis=-1, keepdims=True)
        m_old = m_ref[...]
        m_fin = jnp.maximum(m_old, s_new)
        alpha = jnp.exp(m_old - m_fin)
        e_new = jnp.exp(s_new - m_fin)
        l = alpha * l_ref[...] + e_new
        o = (alpha * acc_ref[...] + e_new * vn) / l
        lam = _lambda_from(lam_ref, lam_init)
        d = o[:, 0:1, :] - lam * o[:, 1:2, :]
        o_ref[...] = _rms(d, sg_ref[...]) * (1.0 - lam_init)


def _diff_attn_sample(page_table, qm, cache_k, cache_v, layer, bias, slope_col, kn, vn, lam_vecs, subln, lam_init):
    n_seq, n_pages = page_table.shape
    page = cache_k.shape[3]
    past = n_pages * page
    pt = page_table.reshape(-1)
    per_seq = lambda rows: pl.BlockSpec((None, N_HEADS, rows, HEAD_DIM), lambda b, p, pt: (b, 0, 0, 0))
    kv = [pl.BlockSpec((None, None, N_HEADS, page, HEAD_DIM),
                       lambda b, p, pt, j=j: (layer, pt[b * n_pages + p * PAGES_PER_STEP + j], 0, 0, 0))
          for j in range(PAGES_PER_STEP)]
    const = lambda shape: pl.BlockSpec(shape, lambda b, p, pt: (0,) * len(shape))
    stat = (N_HEADS, MAP_ROWS, 1)
    grid_spec = pltpu.PrefetchScalarGridSpec(
        num_scalar_prefetch=1,
        grid=(n_seq, n_pages // PAGES_PER_STEP),
        in_specs=[per_seq(MAP_ROWS), *kv, *kv, const((N_HEADS, MAP_ROWS, PAGES_PER_STEP * page)), const(stat),
                  per_seq(1), per_seq(1), const((4, QK_DIM)), const((1, HEAD_DIM))],
        out_specs=per_seq(1),
        scratch_shapes=[pltpu.VMEM(stat, F32), pltpu.VMEM(stat, F32),
                        pltpu.VMEM((N_HEADS, MAP_ROWS, HEAD_DIM), F32)])
    return pl.pallas_call(
        functools.partial(_diff_attn_sample_kernel, page=page, past=past, lam_init=lam_init),
        grid_spec=grid_spec,
        out_shape=jax.ShapeDtypeStruct((n_seq, N_HEADS, 1, HEAD_DIM), F32),
        compiler_params=_cparams("parallel", "arbitrary"),
        name="diff_attn_sample",
    )(pt, qm, *([cache_k] * PAGES_PER_STEP), *([cache_v] * PAGES_PER_STEP), bias, slope_col, kn, vn, lam_vecs, subln)


def kernel(x_prompt, x_sample, mem_prompt, cache_mem_k, cache_mem_v, cache_k, cache_v, state_ffn_conv, page_table, norm_mix, norm_mem, norm_ffn, w_in_a, v_norm_a, spatial_w_a, spatial_b_a, w_out_a, w_in_b, q_norm_b, k_norm_b, lambda_q1, lambda_k1, lambda_q2, lambda_k2, subln_b, w_out_b, w_mem_kv, mem_q_norm, mem_k_norm, w_up, conv_w, conv_b, w_down):
    n_seq_p, seq, _ = x_prompt.shape
    n_seq_s = x_sample.shape[0]
    depth = norm_mix.shape[0]
    n_mem = mem_prompt.shape[1]
    m_p = n_seq_p * seq
    page = cache_k.shape[2]

    xp = x_prompt.reshape(m_p, D_MODEL)
    xs = jnp.pad(x_sample.reshape(n_seq_s, D_MODEL), ((0, SAMPLE_ROWS - n_seq_s), (0, 0)))
    mem = mem_prompt.reshape(n_seq_p * n_mem, D_MODEL)
    cmk = cache_mem_k.reshape(depth, n_seq_s, n_mem, MEM_WIDTH)
    cmv = cache_mem_v.reshape(depth, n_seq_s, n_mem, MEM_WIDTH)
    ck = cache_k.transpose(0, 1, 3, 2, 4)
    cv = cache_v.transpose(0, 1, 3, 2, 4)
    conv_state = jnp.pad(state_ffn_conv, ((0, 0), (0, SAMPLE_ROWS - n_seq_s), (0, 0), (0, 0)))

    slopes = 2.0 ** (-8.0 * jnp.arange(1, N_HEADS + 1, dtype=F32) / N_HEADS)
    slope_col = jnp.broadcast_to(slopes[:, None, None], (N_HEADS, MAP_ROWS, 1))
    step_pos = jnp.arange(PAGES_PER_STEP * page, dtype=F32)
    sample_bias = jnp.broadcast_to(slopes[:, None, None] * step_pos[None, None, :],
                                   (N_HEADS, MAP_ROWS, PAGES_PER_STEP * page))
    lane = jnp.arange(HEAD_DIM)
    map_mask = jnp.stack([lane < QK_DIM, lane >= QK_DIM]).astype(F32)
    head_view = lambda a: a[:n_seq_s].reshape(n_seq_s, N_HEADS, 1, HEAD_DIM)

    tril = jnp.tril(jnp.ones((CHUNK, CHUNK), bool))
    row2 = lambda v: v.reshape(1, -1)

    mem_k_p, mem_v_p, k_p, v_p, conv_p = [], [], [], [], []
    k_s, v_s, conv_s, chunk_v_s = [], [], [], []

    for i in range(depth):
        mk, mv = _in_proj(
            mem, row2(norm_mem[i]), w_mem_kv, i,
            [dict(width=MEM_WIDTH, kind="gnorm", gain=0, f32="flat"),
             dict(width=MEM_WIDTH, kind="plain", f32="flat")],
            [mem_k_norm[i]], MEM_HEAD_DIM, tm=n_seq_p * n_mem, tn=MEM_WIDTH)
        mem_k_p.append(mk)
        mem_v_p.append(mv)

        if i % 2 == 0:
            la = i // 2
            w_out, wl = w_out_a, la
            regions = [dict(width=MIX_WIDTH, kind="gelu", f32="flat"),
                       dict(width=MIX_WIDTH, kind="gelu", f32="flat"),
                       dict(width=MEM_WIDTH, kind="plain", f32="flat")]
            wt = jnp.where(tril, spatial_w_a[la], 0.0).astype(BF16)
            bs = jnp.broadcast_to(spatial_b_a[la][:, :, None], (N_GROUPS, CHUNK, CHUNK))

            u, v, qm_s, w_in = _in_proj(xs, row2(norm_mix[i]), w_in_a, wl, regions, [], QK_DIM, tm=SAMPLE_ROWS,
                                        tn=512, emit_w=True)
            w0 = jnp.repeat(wt[:, 0, 0].astype(F32), CHUNK).reshape(1, MIX_WIDTH)
            b0 = jnp.repeat(spatial_b_a[la][:, 0], CHUNK).reshape(1, MIX_WIDTH)
            mix_s, vrow = _mixer_a_sample(u, v, row2(v_norm_a[la]), w0, b0)
            chunk_v_s.append(vrow[:n_seq_s])

            u, v, qm_p = _in_proj(xp, row2(norm_mix[i]), w_in, None, regions, [], QK_DIM, tm=1024, tn=512,
                                  x_buffers=1)
            mix_p = _mixer_a(u, v, row2(v_norm_a[la]), wt, bs, tm=512)
        else:
            lb = i // 2
            lam_init = 0.8 - 0.6 * math.exp(-0.3 * i)
            w_out, wl = w_out_b, lb
            lam_vecs = jnp.stack([lambda_q1[lb], lambda_k1[lb], lambda_q2[lb], lambda_k2[lb]])
            subln = row2(subln_b[lb])
            gains = [q_norm_b[lb], k_norm_b[lb]]
            qk_scale = QK_DIM ** -0.5

            regions = [dict(width=MIX_WIDTH, kind="gnorm", gain=0, f32="flat"),
                       dict(width=MIX_WIDTH, kind="gnorm", gain=1, f32="flat"),
                       dict(width=MIX_WIDTH, kind="plain", f32="flat"),
                       dict(width=MEM_WIDTH, kind="plain", f32="flat")]
            q, kf, vf, qm_s, w_in = _in_proj(xs, row2(norm_mix[i]), w_in_b, wl, regions, gains, QK_DIM,
                                             tm=SAMPLE_ROWS, tn=512, emit_w=True)
            k_s.append(kf[:n_seq_s])
            v_s.append(vf[:n_seq_s])
            qm = jnp.pad(head_view(q * qk_scale) * map_mask[None, None],
                         ((0, 0), (0, 0), (0, MAP_ROWS - 2), (0, 0))).astype(BF16)
            o = _diff_attn_sample(page_table, qm, ck, cv, lb, sample_bias, slope_col, head_view(kf), head_view(vf),
                                  lam_vecs, subln, lam_init)
            mix_s = jnp.pad(o.reshape(n_seq_s, MIX_WIDTH), ((0, SAMPLE_ROWS - n_seq_s), (0, 0))).astype(BF16)

            regions = [dict(width=MIX_WIDTH, kind="gnorm", gain=0, bf16="cols", scale=qk_scale),
                       dict(width=MIX_WIDTH, kind="gnorm", gain=1, f32="rows", bf16="rows"),
                       dict(width=MIX_WIDTH, kind="plain", f32="rows", bf16="cols"),
                       dict(width=MEM_WIDTH, kind="plain", f32="flat")]
            qt, kf, kb, vf, vt, qm_p = _in_proj(xp, row2(norm_mix[i]), w_in, None, regions, gains, QK_DIM, tm=1024,
                                                tn=512, seq=seq, x_buffers=1)
            k_p.append(kf)
            v_p.append(vf)
            mix_p = _diff_attn_prompt(qt, kb, vt, slopes, lam_vecs, subln, lam_init, n_seq_p, seq, t=1024)

        mo_s = _mem_attend_sample(qm_s, row2(mem_q_norm[i]), cmk, cmv, i, n_seq_s)
        xs, w_out_r = _out_proj_sample(mix_s, mo_s, w_out, wl, xs, tn=512)
        mo_p = _mem_attend(qm_p, row2(mem_q_norm[i]), mk, mv, rows_per_batch=seq, tm=512)
        xp = _out_proj(mix_p, mo_p, w_out_r, xp, tm=1024, tn=1024)

        xs, a_s, w_gate, w_lin, w_down_r = _ffn_sample(
            xs, row2(norm_ffn[i]), w_up, conv_w[i], row2(conv_b[i]), conv_state[i, :, 0], conv_state[i, :, 1],
            w_down, i, tf=512)
        xp, cp = _ffn_prompt(xp, row2(norm_ffn[i]), w_gate, w_lin, conv_w[i], row2(conv_b[i]), w_down_r, seq,
                             tm=1024, tf=512)
        conv_p.append(cp)
        conv_s.append(jnp.stack([state_ffn_conv[i, :, 1], a_s[:n_seq_s]], axis=1))

    heads = lambda a, b, t: a.reshape(b, t, N_HEADS, HEAD_DIM)
    return (xp.reshape(n_seq_p, seq, D_MODEL),
            xs[:n_seq_s].reshape(n_seq_s, 1, D_MODEL),
            jnp.stack(mem_k_p).reshape(depth, n_seq_p, n_mem, MEM_HEADS, MEM_HEAD_DIM),
            jnp.stack(mem_v_p).reshape(depth, n_seq_p, n_mem, MEM_HEADS, MEM_HEAD_DIM),
            jnp.stack(k_p).transpose(0, 1, 3, 2, 4),
            jnp.stack(v_p).transpose(0, 1, 3, 2, 4),
            jnp.stack(conv_p),
            jnp.stack([heads(a, n_seq_s, 1) for a in k_s]),
            jnp.stack([heads(a, n_seq_s, 1) for a in v_s]),
            jnp.stack(conv_s),
            jnp.stack(chunk_v_s).reshape(len(chunk_v_s), n_seq_s, 1, MIX_WIDTH))
```

```python
import functools
import math

import jax
import jax.numpy as jnp
from jax import lax
from jax.experimental import pallas as pl
from jax.experimental.pallas import tpu as pltpu

F32 = jnp.float32
BF16 = jnp.bfloat16

D_MODEL = 2048
MEM_HEADS = 4
MEM_HEAD_DIM = 128
MEM_WIDTH = MEM_HEADS * MEM_HEAD_DIM
MIX_WIDTH = D_MODEL - MEM_WIDTH
CHUNK = 128
N_GROUPS = 12
N_HEADS = 12
HEAD_DIM = 128
QK_DIM = 64
D_FF = 5632
CONV_W = 3
EPS = 1e-6
NEG_INF = -1e30
SAMPLE_ROWS = 16
MXU_WIDTH = 256

VMEM_LIMIT = 56 * 1024 * 1024


def _cparams(*sem):
    return pltpu.CompilerParams(dimension_semantics=sem, vmem_limit_bytes=VMEM_LIMIT)


def _rms(x, g):
    return x * lax.rsqrt(jnp.mean(x * x, axis=-1, keepdims=True) + EPS) * g


def _dot(a, b):
    return jnp.dot(a, b, preferred_element_type=F32)


def _dot_nt(a, b):
    return lax.dot_general(a, b, (((1,), (1,)), ((), ())), preferred_element_type=F32)


def _block_diag_ones(n, group):
    r = jnp.arange(n) // group
    return (r[:, None] == r[None, :]).astype(BF16)


def _in_proj_kernel(x_ref, g_ref, w_ref, bd_ref, *refs, regions, n_gain, group, emit_w):
    gain_refs = refs[:n_gain]
    n_out = sum((r["f32"] is not None) + (r["bf16"] is not None) for r in regions)
    out_refs = refs[n_gain:n_gain + n_out]
    wcopy_ref = refs[n_gain + n_out] if emit_w else None
    h_ref = refs[-1]
    j = pl.program_id(1)

    @pl.when(j == 0)
    def _():
        h_ref[...] = _rms(x_ref[...], g_ref[...]).astype(BF16)

    def weight():
        w = w_ref[...].astype(BF16)
        if emit_w:
            wcopy_ref[...] = w
        return w

    def store(ref, val, layout):
        if layout == "flat":
            ref[...] = val.astype(ref.dtype)
            return
        for hh in range(val.shape[1] // HEAD_DIM):
            head = val[:, hh * HEAD_DIM:(hh + 1) * HEAD_DIM]
            ref[hh] = head.T.astype(ref.dtype) if layout == "cols" else head.astype(ref.dtype)

    for r in regions:
        @pl.when((j >= r["start"]) & (j < r["start"] + r["ntiles"]))
        def _(r=r):
            z = _dot(h_ref[...], weight())
            if r["kind"] == "gelu":
                y = jax.nn.gelu(z)
            elif r["kind"] == "plain":
                y = z
            else:
                z2 = (z * z).astype(BF16)
                bw = bd_ref.shape[0]
                ss = jnp.concatenate([_dot(z2[:, c:c + bw], bd_ref[...]) for c in range(0, z2.shape[1], bw)], axis=1)
                y = z * lax.rsqrt(ss * (1.0 / group) + EPS) * gain_refs[r["gain"]][...]
            if r["f32"] is not None:
                store(out_refs[r["f32"]], y, r["f32_layout"])
            if r["bf16"] is not None:
                store(out_refs[r["bf16"]], y if r["scale"] == 1.0 else y * r["scale"], r["bf16_layout"])


def _in_proj(x, g, w, layer, regions, gains, group, tm, tn, seq=None, x_buffers=2, emit_w=False):
    m, d = x.shape
    assert not emit_w or m == tm
    regs, out_shapes, out_specs = [], [], []
    start = 0
    for r in regions:
        nt = r["width"] // tn
        reg = dict(start=start, ntiles=nt, kind=r["kind"], gain=r.get("gain"), scale=r.get("scale", 1.0),
                   f32=None, bf16=None)
        for key, dt in (("f32", F32), ("bf16", BF16)):
            layout = r.get(key)
            reg[key + "_layout"] = layout
            if layout is None:
                continue
            reg[key] = len(out_shapes)
            tile = lambda i, j, s=start, n=nt: jnp.clip(j - s, 0, n - 1)
            if layout == "flat":
                out_shapes.append(jax.ShapeDtypeStruct((m, r["width"]), dt))
                out_specs.append(pl.BlockSpec((tm, tn), lambda i, j, tile=tile: (i, tile(i, j))))
                continue
            tps = seq // tm
            heads, hpt = r["width"] // HEAD_DIM, tn // HEAD_DIM
            if layout == "rows":
                out_shapes.append(jax.ShapeDtypeStruct((m // seq, heads, seq, HEAD_DIM), dt))
                out_specs.append(pl.BlockSpec(
                    (None, hpt, tm, HEAD_DIM), lambda i, j, tile=tile, tps=tps: (i // tps, tile(i, j), i % tps, 0)))
            else:
                out_shapes.append(jax.ShapeDtypeStruct((m // seq, heads, HEAD_DIM, seq), dt))
                out_specs.append(pl.BlockSpec(
                    (None, hpt, HEAD_DIM, tm), lambda i, j, tile=tile, tps=tps: (i // tps, tile(i, j), 0, i % tps)))
        regs.append(reg)
        start += nt
    assert start * tn == w.shape[-1]
    if emit_w:
        out_shapes.append(jax.ShapeDtypeStruct((d, w.shape[-1]), BF16))
        out_specs.append(pl.BlockSpec((d, tn), lambda i, j: (0, j)))
    if w.ndim == 2:
        w_spec = pl.BlockSpec((d, tn), lambda i, j: (0, j))
    else:
        w_spec = pl.BlockSpec((None, d, tn), lambda i, j: (layer, 0, j))
    gain_tiles = [jnp.tile(gv.reshape(1, -1), (1, tn // gv.size)) for gv in gains]
    in_specs = [
        pl.BlockSpec((tm, d), lambda i, j: (i, 0), pipeline_mode=pl.Buffered(x_buffers)),
        pl.BlockSpec((1, d), lambda i, j: (0, 0)),
        w_spec,
        pl.BlockSpec((MXU_WIDTH, MXU_WIDTH), lambda i, j: (0, 0)),
    ] + [pl.BlockSpec((1, tn), lambda i, j: (0, 0)) for _ in gain_tiles]
    return pl.pallas_call(
        functools.partial(_in_proj_kernel, regions=regs, n_gain=len(gain_tiles), group=group, emit_w=emit_w),
        grid=(m // tm, start),
        in_specs=in_specs,
        out_specs=out_specs,
        out_shape=out_shapes,
        scratch_shapes=[pltpu.VMEM((tm, d), BF16)],
        compiler_params=_cparams("parallel", "arbitrary"),
        name="in_proj",
    )(x, g, w, _block_diag_ones(MXU_WIDTH, group), *gain_tiles)


def _mixer_a_kernel(u_ref, v_ref, vg_ref, wt_ref, bs_ref, o_ref, *, n_chunks):
    vb = _rms(v_ref[...], vg_ref[...]).astype(BF16)
    for g in range(N_GROUPS):
        cols = slice(g * CHUNK, (g + 1) * CHUNK)
        rhs = jnp.concatenate([vb[c * CHUNK:(c + 1) * CHUNK, cols] for c in range(n_chunks)], axis=1)
        mixed = _dot(wt_ref[g], rhs)
        for c in range(n_chunks):
            rows = slice(c * CHUNK, (c + 1) * CHUNK)
            mc = mixed[:, c * CHUNK:(c + 1) * CHUNK] + bs_ref[g]
            o_ref[rows, cols] = (u_ref[rows, cols] * mc).astype(BF16)


def _mixer_a(u, v, vgain, wt, bs, tm):
    m = u.shape[0]
    row = pl.BlockSpec((tm, MIX_WIDTH), lambda i: (i, 0))
    return pl.pallas_call(
        functools.partial(_mixer_a_kernel, n_chunks=tm // CHUNK),
        grid=(m // tm,),
        in_specs=[row, row,
                  pl.BlockSpec((1, MIX_WIDTH), lambda i: (0, 0)),
                  pl.BlockSpec((N_GROUPS, CHUNK, CHUNK), lambda i: (0, 0, 0)),
                  pl.BlockSpec((N_GROUPS, CHUNK, CHUNK), lambda i: (0, 0, 0))],
        out_specs=row,
        out_shape=jax.ShapeDtypeStruct((m, MIX_WIDTH), BF16),
        compiler_params=_cparams("parallel"),
        name="mixer_a",
    )(u, v, vgain, wt, bs)


def _mixer_a_sample_kernel(u_ref, v_ref, vg_ref, w0_ref, b0_ref, o_ref, vn_ref):
    vn = _rms(v_ref[...], vg_ref[...])
    vn_ref[...] = vn
    mixed = vn.astype(BF16).astype(F32) * w0_ref[...] + b0_ref[...]
    o_ref[...] = (u_ref[...] * mixed).astype(BF16)


def _mixer_a_sample(u, v, vgain, w0, b0):
    m = u.shape[0]
    full = pl.BlockSpec((m, MIX_WIDTH), lambda: (0, 0))
    vec = pl.BlockSpec((1, MIX_WIDTH), lambda: (0, 0))
    return pl.pallas_call(
        _mixer_a_sample_kernel,
        in_specs=[full, full, vec, vec, vec],
        out_specs=[full, full],
        out_shape=[jax.ShapeDtypeStruct((m, MIX_WIDTH), BF16), jax.ShapeDtypeStruct((m, MIX_WIDTH), F32)],
        name="mixer_a_sample",
    )(u, v, vgain, w0, b0)


def _mem_attend_tile(q, qg, k, v):
    outs = []
    for h in range(MEM_HEADS):
        cols = slice(h * MEM_HEAD_DIM, (h + 1) * MEM_HEAD_DIM)
        qn = _rms(q[:, cols], qg).astype(BF16)
        s = _dot_nt(qn, k[:, cols].astype(BF16)) * (MEM_HEAD_DIM ** -0.5)
        s = s - jnp.max(s, axis=-1, keepdims=True)
        e = jnp.exp(s)
        p = e / jnp.sum(e, axis=-1, keepdims=True)
        outs.append(_dot(p.astype(BF16), v[:, cols].astype(BF16)))
    return jnp.concatenate(outs, axis=1)


def _mem_attend_kernel(q_ref, qg_ref, k_ref, v_ref, o_ref):
    o_ref[...] = _mem_attend_tile(q_ref[...], qg_ref[...], k_ref[...], v_ref[...]).astype(BF16)


def _mem_attend(q, qg, k, v, rows_per_batch, tm):
    m = q.shape[0]
    n_mem = k.shape[0] // (m // rows_per_batch)
    tiles_per_batch = rows_per_batch // tm
    kv = pl.BlockSpec((n_mem, MEM_WIDTH), lambda i: (i // tiles_per_batch, 0))
    return pl.pallas_call(
        _mem_attend_kernel,
        grid=(m // tm,),
        in_specs=[pl.BlockSpec((tm, MEM_WIDTH), lambda i: (i, 0)),
                  pl.BlockSpec((1, MEM_HEAD_DIM), lambda i: (0, 0)), kv, kv],
        out_specs=pl.BlockSpec((tm, MEM_WIDTH), lambda i: (i, 0)),
        out_shape=jax.ShapeDtypeStruct((m, MEM_WIDTH), BF16),
        compiler_params=_cparams("parallel"),
        name="mem_attend",
    )(q, qg, k, v)


def _mem_attend_sample_kernel(q_ref, qg_ref, k_ref, v_ref, o_ref):
    b = pl.program_id(0)
    o = _mem_attend_tile(q_ref[...], qg_ref[...], k_ref[...], v_ref[...]).astype(BF16)

    @pl.when(b == 0)
    def _():
        o_ref[...] = jnp.zeros_like(o_ref)

    row = lax.broadcasted_iota(jnp.int32, o.shape, 0)
    o_ref[...] = jnp.where(row == b, o, o_ref[...])


def _mem_attend_sample(q, qg, k, v, layer, n_seq):
    m = q.shape[0]
    n_mem = k.shape[2]
    kv = pl.BlockSpec((None, None, n_mem, MEM_WIDTH), lambda b: (layer, b, 0, 0))
    full = pl.BlockSpec((m, MEM_WIDTH), lambda b: (0, 0))
    return pl.pallas_call(
        _mem_attend_sample_kernel,
        grid=(n_seq,),
        in_specs=[full, pl.BlockSpec((1, MEM_HEAD_DIM), lambda b: (0, 0)), kv, kv],
        out_specs=full,
        out_shape=jax.ShapeDtypeStruct((m, MEM_WIDTH), BF16),
        compiler_params=_cparams("arbitrary"),
        name="mem_attend_sample",
    )(q, qg, k, v)


def _out_proj_kernel(a_ref, b_ref, wa_ref, wb_ref, x_ref, o_ref):
    o_ref[...] = x_ref[...] + _dot(a_ref[...], wa_ref[...]) + _dot(b_ref[...], wb_ref[...])


def _out_proj(mix, mo, w, x, tm, tn):
    m = x.shape[0]
    assert MIX_WIDTH % MEM_WIDTH == 0
    return pl.pallas_call(
        _out_proj_kernel,
        grid=(m // tm, D_MODEL // tn),
        in_specs=[pl.BlockSpec((tm, MIX_WIDTH), lambda i, j: (i, 0)),
                  pl.BlockSpec((tm, MEM_WIDTH), lambda i, j: (i, 0)),
                  pl.BlockSpec((MIX_WIDTH, tn), lambda i, j: (0, j)),
                  pl.BlockSpec((MEM_WIDTH, tn), lambda i, j: (MIX_WIDTH // MEM_WIDTH, j)),
                  pl.BlockSpec((tm, tn), lambda i, j: (i, j))],
        out_specs=pl.BlockSpec((tm, tn), lambda i, j: (i, j)),
        out_shape=jax.ShapeDtypeStruct((m, D_MODEL), F32),
        compiler_params=_cparams("parallel", "arbitrary"),
        name="out_proj",
    )(mix, mo, w, w, x)


def _out_proj_sample_kernel(a_ref, b_ref, w_ref, x_ref, o_ref, wcopy_ref):
    w = w_ref[...].astype(BF16)
    wcopy_ref[...] = w
    o_ref[...] = x_ref[...] + _dot(a_ref[...], w[:MIX_WIDTH]) + _dot(b_ref[...], w[MIX_WIDTH:])


def _out_proj_sample(mix, mo, w, layer, x, tn):
    m = x.shape[0]
    return pl.pallas_call(
        _out_proj_sample_kernel,
        grid=(D_MODEL // tn,),
        in_specs=[pl.BlockSpec((m, MIX_WIDTH), lambda j: (0, 0)),
                  pl.BlockSpec((m, MEM_WIDTH), lambda j: (0, 0)),
                  pl.BlockSpec((None, D_MODEL, tn), lambda j: (layer, 0, j)),
                  pl.BlockSpec((m, tn), lambda j: (0, j))],
        out_specs=[pl.BlockSpec((m, tn), lambda j: (0, j)),
                   pl.BlockSpec((D_MODEL, tn), lambda j: (0, j))],
        out_shape=[jax.ShapeDtypeStruct((m, D_MODEL), F32), jax.ShapeDtypeStruct((D_MODEL, D_MODEL), BF16)],
        compiler_params=_cparams("arbitrary"),
        name="out_proj_sample",
    )(mix, mo, w, x)


HALO = 16


def _ffn_prompt_kernel(x_ref, halo_ref, g_ref, wa_ref, wb_ref, cw_ref, cb_ref, wd_ref, o_ref, conv_ref,
                       h_ref, a_ref, *, tm, tiles_per_seq):
    i = pl.program_id(0)
    f = pl.program_id(1)

    @pl.when(f == 0)
    def _():
        h_ref[HALO:, :] = _rms(x_ref[...], g_ref[...]).astype(BF16)
        hh = _rms(halo_ref[...], g_ref[...])
        h_ref[:HALO, :] = jnp.where(i % tiles_per_seq == 0, 0.0, hh).astype(BF16)
        o_ref[...] = x_ref[...]

    a_ref[...] = _dot(h_ref[...], wa_ref[...])
    b = _dot(h_ref[HALO:, :], wb_ref[...])
    c = cb_ref[...]
    for tap in range(CONV_W):
        off = HALO - (CONV_W - 1) + tap
        c = c + a_ref[off:off + tm, :] * cw_ref[tap:tap + 1, :]
    act = (jax.nn.silu(c) * b).astype(BF16)
    half = D_MODEL // 2
    o_ref[:, :half] += _dot(act, wd_ref[:, :half])
    o_ref[:, half:] += _dot(act, wd_ref[:, half:])
    conv_ref[...] = a_ref[tm + HALO - 8:tm + HALO, :]


def _ffn_prompt(x, g, w_gate, w_lin, cw, cb, w_down, seq, tm, tf):
    m = x.shape[0]
    nf = D_FF // tf
    tiles_per_seq = seq // tm
    halo_blocks = tm // HALO
    once = pl.Buffered(1)
    out, conv = pl.pallas_call(
        functools.partial(_ffn_prompt_kernel, tm=tm, tiles_per_seq=tiles_per_seq),
        grid=(m // tm, nf),
        in_specs=[pl.BlockSpec((tm, D_MODEL), lambda i, f: (i, 0)),
                  pl.BlockSpec((HALO, D_MODEL), lambda i, f: (jnp.maximum(i * halo_blocks - 1, 0), 0)),
                  pl.BlockSpec((1, D_MODEL), lambda i, f: (0, 0)),
                  pl.BlockSpec((D_MODEL, tf), lambda i, f: (0, f)),
                  pl.BlockSpec((D_MODEL, tf), lambda i, f: (0, f)),
                  pl.BlockSpec((CONV_W, tf), lambda i, f: (0, f)),
                  pl.BlockSpec((1, tf), lambda i, f: (0, f)),
                  pl.BlockSpec((tf, D_MODEL), lambda i, f: (f, 0))],
        out_specs=[pl.BlockSpec((tm, D_MODEL), lambda i, f: (i, 0), pipeline_mode=once),
                   pl.BlockSpec((None, 8, tf), lambda i, f: (i, 0, f))],
        out_shape=[jax.ShapeDtypeStruct((m, D_MODEL), F32),
                   jax.ShapeDtypeStruct((m // tm, 8, D_FF), F32)],
        scratch_shapes=[pltpu.VMEM((tm + HALO, D_MODEL), BF16), pltpu.VMEM((tm + HALO, tf), F32)],
        compiler_params=_cparams("parallel", "arbitrary"),
        name="ffn_prompt",
    )(x, x, g, w_gate, w_lin, cw, cb, w_down)
    return out, conv[tiles_per_seq - 1::tiles_per_seq, 8 - (CONV_W - 1):, :]


def _ffn_sample_kernel(x_ref, g_ref, wa_ref, wb_ref, cw_ref, cb_ref, p0_ref, p1_ref, wd_ref, o_ref, a_out_ref,
                       wa_copy_ref, wb_copy_ref, wd_copy_ref, h_ref):
    f = pl.program_id(0)

    @pl.when(f == 0)
    def _():
        h_ref[...] = _rms(x_ref[...], g_ref[...]).astype(BF16)

    wa = wa_ref[...].astype(BF16)
    wb = wb_ref[...].astype(BF16)
    wd = wd_ref[...].astype(BF16)
    wa_copy_ref[...] = wa
    wb_copy_ref[...] = wb
    wd_copy_ref[...] = wd
    a = _dot(h_ref[...], wa)
    b = _dot(h_ref[...], wb)
    a_out_ref[...] = a
    c = cb_ref[...] + p0_ref[...] * cw_ref[0:1, :] + p1_ref[...] * cw_ref[1:2, :] + a * cw_ref[2:3, :]
    y = _dot((jax.nn.silu(c) * b).astype(BF16), wd)

    @pl.when(f == 0)
    def _():
        o_ref[...] = x_ref[...] + y

    @pl.when(f > 0)
    def _():
        o_ref[...] += y


def _ffn_sample(x, g, w_up, cw, cb, p0, p1, w_down, layer, tf):
    m = x.shape[0]
    nf = D_FF // tf
    up_tile = pl.BlockSpec((D_MODEL, tf), lambda f: (0, f))
    down_tile = pl.BlockSpec((tf, D_MODEL), lambda f: (f, 0))
    full = pl.BlockSpec((m, D_MODEL), lambda f: (0, 0))
    col = pl.BlockSpec((m, tf), lambda f: (0, f))
    return pl.pallas_call(
        _ffn_sample_kernel,
        grid=(nf,),
        in_specs=[full,
                  pl.BlockSpec((1, D_MODEL), lambda f: (0, 0)),
                  pl.BlockSpec((None, D_MODEL, tf), lambda f: (layer, 0, f)),
                  pl.BlockSpec((None, D_MODEL, tf), lambda f: (layer, 0, nf + f)),
                  pl.BlockSpec((CONV_W, tf), lambda f: (0, f)),
                  pl.BlockSpec((1, tf), lambda f: (0, f)),
                  col, col,
                  pl.BlockSpec((None, tf, D_MODEL), lambda f: (layer, f, 0))],
        out_specs=[full, col, up_tile, up_tile, down_tile],
        out_shape=[jax.ShapeDtypeStruct((m, D_MODEL), F32), jax.ShapeDtypeStruct((m, D_FF), F32),
                   jax.ShapeDtypeStruct((D_MODEL, D_FF), BF16), jax.ShapeDtypeStruct((D_MODEL, D_FF), BF16),
                   jax.ShapeDtypeStruct((D_FF, D_MODEL), BF16)],
        scratch_shapes=[pltpu.VMEM((m, D_MODEL), BF16)],
        compiler_params=_cparams("arbitrary"),
        name="ffn_sample",
    )(x, g, w_up, w_up, cw, cb, p0, p1, w_down)


def _lambda_from(lam_ref, lam_init):
    v = lam_ref[...]
    a = jnp.sum(v[0:1] * v[1:2], axis=-1, keepdims=True)
    b = jnp.sum(v[2:3] * v[3:4], axis=-1, keepdims=True)
    return jnp.exp(a) - jnp.exp(b) + lam_init


HEADS_PER_STEP = 4
AUG_ROWS = 16
POS_SPLIT = 32


def _diff_attn_prompt_kernel(qi_ref, ki_ref, slope_ref, qt_ref, k_ref, vt_ref, kfeat_ref, srows_ref, ones_ref,
                             lam_ref, sg_ref, o_ref, qs_ref, m_ref, acc_ref, *, t, lam_init):
    hp = pl.program_id(1)
    step = pl.program_id(2)
    qi = qi_ref[step]
    ki = ki_ref[step]
    heads = range(HEADS_PER_STEP)

    @pl.when(ki == 0)
    def _():
        for hh in heads:
            qt = qt_ref[hh]
            dim = lax.broadcasted_iota(jnp.int32, qt.shape, 0)
            zero = jnp.zeros_like(qt)
            qs_ref[hh, :HEAD_DIM, :t] = jnp.where(dim < QK_DIM, qt, zero)
            qs_ref[hh, :HEAD_DIM, t:] = jnp.where(dim >= QK_DIM, qt, zero)
            qs_ref[hh, HEAD_DIM:HEAD_DIM + AUG_ROWS, :] = srows_ref[hh]
            qs_ref[hh, HEAD_DIM + AUG_ROWS:, :] = jnp.zeros((HEAD_DIM - AUG_ROWS, 2 * t), BF16)
        m_ref[...] = jnp.full_like(m_ref, NEG_INF)
        acc_ref[...] = jnp.zeros_like(acc_ref)

    def update(masked):
        for hh in heads:
            slope = slope_ref[hp * HEADS_PER_STEP + hh]
            s = _dot(jnp.concatenate([k_ref[hh], kfeat_ref[...]], axis=1), qs_ref[hh])
            if masked:
                key = lax.broadcasted_iota(jnp.int32, (t, t), 0)
                qry = lax.broadcasted_iota(jnp.int32, (t, t), 1)
                keep = key <= qry
                s = jnp.concatenate([jnp.where(keep, s[:, :t], NEG_INF), jnp.where(keep, s[:, t:], NEG_INF)],
                                    axis=1)
            off = slope * ((ki - qi) * t).astype(F32)
            m_old = m_ref[hh]
            m_new = jnp.maximum(m_old, jnp.max(s, axis=0, keepdims=True) + off)
            p = jnp.exp(s - (m_new - off)).astype(BF16)
            alpha = jnp.exp(m_old - m_new)
            vt_aug = jnp.concatenate([vt_ref[hh], ones_ref[...]], axis=0)
            acc_ref[hh] = alpha * acc_ref[hh] + _dot(vt_aug, p)
            m_ref[hh] = m_new

    @pl.when(ki < qi)
    def _():
        update(False)

    @pl.when(ki == qi)
    def _():
        update(True)
        lam = _lambda_from(lam_ref, lam_init)
        for hh in heads:
            acc = acc_ref[hh]
            inv = 1.0 / acc[HEAD_DIM:HEAD_DIM + 1, :]
            o = acc[:HEAD_DIM, :t] * inv[:, :t] - lam * (acc[:HEAD_DIM, t:] * inv[:, t:])
            on = o * lax.rsqrt(jnp.mean(o * o, axis=0, keepdims=True) + EPS)
            o_ref[:, hh * HEAD_DIM:(hh + 1) * HEAD_DIM] = (on.T * sg_ref[...] * (1.0 - lam_init)).astype(BF16)


def _diff_attn_prompt(qt, k, vt, slopes, lam_vecs, subln, lam_init, n_seq, seq, t):
    m = n_seq * seq
    nt = seq // t
    assert t <= POS_SPLIT * POS_SPLIT
    pairs = [(a, b) for a in range(nt) for b in range(a + 1)]
    qi = jnp.asarray([p[0] for p in pairs], jnp.int32)
    ki = jnp.asarray([p[1] for p in pairs], jnp.int32)
    top16 = lambda x: lax.bitcast_convert_type(
        lax.bitcast_convert_type(x, jnp.uint32) & jnp.uint32(0xFFFF0000), F32)
    s1 = top16(slopes)
    s2 = top16(slopes - s1)
    s3 = top16(slopes - s1 - s2)
    pieces = jnp.stack([s1, s2, s3])
    rows = jnp.concatenate([pieces * POS_SPLIT, pieces, jnp.zeros((AUG_ROWS - 6, N_HEADS), F32)]).T
    srows = jnp.broadcast_to(rows[:, :, None], (N_HEADS, AUG_ROWS, 2 * t)).astype(BF16)
    pos = jnp.arange(t)
    digits = jnp.stack([pos // POS_SPLIT] * 3 + [pos % POS_SPLIT] * 3, axis=1)
    kfeat = jnp.pad(digits, ((0, 0), (0, HEAD_DIM - 6))).astype(BF16)
    ones = jnp.zeros((AUG_ROWS, t), BF16).at[0].set(1)
    const = lambda shape: pl.BlockSpec(shape, lambda b, h, s, qi, ki: (0,) * len(shape))
    hps = HEADS_PER_STEP
    qspec = pl.BlockSpec((None, hps, HEAD_DIM, t), lambda b, h, s, qi, ki: (b, h, 0, qi[s]))
    kspec = pl.BlockSpec((None, hps, t, HEAD_DIM), lambda b, h, s, qi, ki: (b, h, ki[s], 0))
    vspec = pl.BlockSpec((None, hps, HEAD_DIM, t), lambda b, h, s, qi, ki: (b, h, 0, ki[s]))
    grid_spec = pltpu.PrefetchScalarGridSpec(
        num_scalar_prefetch=2,
        grid=(n_seq, N_HEADS // hps, len(pairs)),
        in_specs=[pl.BlockSpec(memory_space=pltpu.SMEM), qspec, kspec, vspec,
                  const((t, HEAD_DIM)),
                  pl.BlockSpec((hps, AUG_ROWS, 2 * t), lambda b, h, s, qi, ki: (h, 0, 0)),
                  const((AUG_ROWS, t)), const((4, QK_DIM)), const((1, HEAD_DIM))],
        out_specs=pl.BlockSpec((t, hps * HEAD_DIM), lambda b, h, s, qi, ki: (b * nt + qi[s], h)),
        scratch_shapes=[pltpu.VMEM((hps, 2 * HEAD_DIM, 2 * t), BF16), pltpu.VMEM((hps, 1, 2 * t), F32),
                        pltpu.VMEM((hps, HEAD_DIM + AUG_ROWS, 2 * t), F32)])
    return pl.pallas_call(
        functools.partial(_diff_attn_prompt_kernel, t=t, lam_init=lam_init),
        grid_spec=grid_spec,
        out_shape=jax.ShapeDtypeStruct((m, MIX_WIDTH), BF16),
        compiler_params=_cparams("parallel", "parallel", "arbitrary"),
        name="diff_attn_prompt",
    )(qi, ki, slopes, qt, k, vt, kfeat, srows, ones, lam_vecs, subln)


MAP_ROWS = 16


PAGES_PER_STEP = 8


def _diff_attn_sample_kernel(pt_ref, qm_ref, *refs, page, past, lam_init):
    k_refs = refs[:PAGES_PER_STEP]
    v_refs = refs[PAGES_PER_STEP:2 * PAGES_PER_STEP]
    bias_ref, slope_ref, kn_ref, vn_ref, lam_ref, sg_ref, o_ref, m_ref, l_ref, acc_ref = refs[2 * PAGES_PER_STEP:]
    p = pl.program_id(1)
    n_steps = pl.num_programs(1)

    @pl.when(p == 0)
    def _():
        m_ref[...] = jnp.full_like(m_ref, NEG_INF)
        l_ref[...] = jnp.zeros_like(l_ref)
        acc_ref[...] = jnp.zeros_like(acc_ref)

    qm = qm_ref[...]
    s = jnp.concatenate(
        [jnp.einsum("hmd,hkd->hmk", qm, k_ref[...].astype(BF16), preferred_element_type=F32) for k_ref in k_refs],
        axis=-1)
    s = s + bias_ref[...]
    off = slope_ref[...] * (p * (PAGES_PER_STEP * page) - past).astype(F32)
    m_old = m_ref[...]
    m_new = jnp.maximum(m_old, jnp.max(s, axis=-1, keepdims=True) + off)
    e = jnp.exp(s - (m_new - off))
    alpha = jnp.exp(m_old - m_new)
    l_ref[...] = alpha * l_ref[...] + jnp.sum(e, axis=-1, keepdims=True)
    e = e.astype(BF16)
    pv = None
    for j, v_ref in enumerate(v_refs):
        term = jnp.einsum("hmk,hkd->hmd", e[:, :, j * page:(j + 1) * page], v_ref[...].astype(BF16),
                          preferred_element_type=F32)
        pv = term if pv is None else pv + term
    acc_ref[...] = alpha * acc_ref[...] + pv
    m_ref[...] = m_new

    @pl.when(p == n_steps - 1)
    def _():
        kn = kn_ref[...].astype(BF16).astype(F32)
        vn = vn_ref[...].astype(BF16).astype(F32)
        s_new = jnp.sum(qm.astype(F32) * kn, axis=-1, keepdims=True)
        m_old = m_ref[...]
        m_fin = jnp.maximum(m_old, s_new)
        alpha = jnp.exp(m_old - m_fin)
        e_new = jnp.exp(s_new - m_fin)
        l = alpha * l_ref[...] + e_new
        o = (alpha * acc_ref[...] + e_new * vn) / l
        lam = _lambda_from(lam_ref, lam_init)
        d = o[:, 0:1, :] - lam * o[:, 1:2, :]
        o_ref[...] = _rms(d, sg_ref[...]) * (1.0 - lam_init)


def _diff_attn_sample(page_table, qm, cache_k, cache_v, layer, bias, slope_col, kn, vn, lam_vecs, subln, lam_init):
    n_seq, n_pages = page_table.shape
    page = cache_k.shape[3]
    past = n_pages * page
    pt = page_table.reshape(-1)
    per_seq = lambda rows: pl.BlockSpec((None, N_HEADS, rows, HEAD_DIM), lambda b, p, pt: (b, 0, 0, 0))
    kv = [pl.BlockSpec((None, None, N_HEADS, page, HEAD_DIM),
                       lambda b, p, pt, j=j: (layer, pt[b * n_pages + p * PAGES_PER_STEP + j], 0, 0, 0))
          for j in range(PAGES_PER_STEP)]
    const = lambda shape: pl.BlockSpec(shape, lambda b, p, pt: (0,) * len(shape))
    stat = (N_HEADS, MAP_ROWS, 1)
    grid_spec = pltpu.PrefetchScalarGridSpec(
        num_scalar_prefetch=1,
        grid=(n_seq, n_pages // PAGES_PER_STEP),
        in_specs=[per_seq(MAP_ROWS), *kv, *kv, const((N_HEADS, MAP_ROWS, PAGES_PER_STEP * page)), const(stat),
                  per_seq(1), per_seq(1), const((4, QK_DIM)), const((1, HEAD_DIM))],
        out_specs=per_seq(1),
        scratch_shapes=[pltpu.VMEM(stat, F32), pltpu.VMEM(stat, F32),
                        pltpu.VMEM((N_HEADS, MAP_ROWS, HEAD_DIM), F32)])
    return pl.pallas_call(
        functools.partial(_diff_attn_sample_kernel, page=page, past=past, lam_init=lam_init),
        grid_spec=grid_spec,
        out_shape=jax.ShapeDtypeStruct((n_seq, N_HEADS, 1, HEAD_DIM), F32),
        compiler_params=_cparams("parallel", "arbitrary"),
        name="diff_attn_sample",
    )(pt, qm, *([cache_k] * PAGES_PER_STEP), *([cache_v] * PAGES_PER_STEP), bias, slope_col, kn, vn, lam_vecs, subln)


def kernel(x_prompt, x_sample, mem_prompt, cache_mem_k, cache_mem_v, cache_k, cache_v, state_ffn_conv, page_table, norm_mix, norm_mem, norm_ffn, w_in_a, v_norm_a, spatial_w_a, spatial_b_a, w_out_a, w_in_b, q_norm_b, k_norm_b, lambda_q1, lambda_k1, lambda_q2, lambda_k2, subln_b, w_out_b, w_mem_kv, mem_q_norm, mem_k_norm, w_up, conv_w, conv_b, w_down):
    n_seq_p, seq, _ = x_prompt.shape
    n_seq_s = x_sample.shape[0]
    depth = norm_mix.shape[0]
    n_mem = mem_prompt.shape[1]
    m_p = n_seq_p * seq
    page = cache_k.shape[2]

    xp = x_prompt.reshape(m_p, D_MODEL)
    xs = jnp.pad(x_sample.reshape(n_seq_s, D_MODEL), ((0, SAMPLE_ROWS - n_seq_s), (0, 0)))
    mem = mem_prompt.reshape(n_seq_p * n_mem, D_MODEL)
    cmk = cache_mem_k.reshape(depth, n_seq_s, n_mem, MEM_WIDTH)
    cmv = cache_mem_v.reshape(depth, n_seq_s, n_mem, MEM_WIDTH)
    ck = cache_k.transpose(0, 1, 3, 2, 4)
    cv = cache_v.transpose(0, 1, 3, 2, 4)
    conv_state = jnp.pad(state_ffn_conv, ((0, 0), (0, SAMPLE_ROWS - n_seq_s), (0, 0), (0, 0)))

    slopes = 2.0 ** (-8.0 * jnp.arange(1, N_HEADS + 1, dtype=F32) / N_HEADS)
    slope_col = jnp.broadcast_to(slopes[:, None, None], (N_HEADS, MAP_ROWS, 1))
    step_pos = jnp.arange(PAGES_PER_STEP * page, dtype=F32)
    sample_bias = jnp.broadcast_to(slopes[:, None, None] * step_pos[None, None, :],
                                   (N_HEADS, MAP_ROWS, PAGES_PER_STEP * page))
    lane = jnp.arange(HEAD_DIM)
    map_mask = jnp.stack([lane < QK_DIM, lane >= QK_DIM]).astype(F32)
    head_view = lambda a: a[:n_seq_s].reshape(n_seq_s, N_HEADS, 1, HEAD_DIM)

    tril = jnp.tril(jnp.ones((CHUNK, CHUNK), bool))
    row2 = lambda v: v.reshape(1, -1)

    mem_k_p, mem_v_p, k_p, v_p, conv_p = [], [], [], [], []
    k_s, v_s, conv_s, chunk_v_s = [], [], [], []

    for i in range(depth):
        mk, mv = _in_proj(
            mem, row2(norm_mem[i]), w_mem_kv, i,
            [dict(width=MEM_WIDTH, kind="gnorm", gain=0, f32="flat"),
             dict(width=MEM_WIDTH, kind="plain", f32="flat")],
            [mem_k_norm[i]], MEM_HEAD_DIM, tm=n_seq_p * n_mem, tn=MEM_WIDTH)
        mem_k_p.append(mk)
        mem_v_p.append(mv)

        if i % 2 == 0:
            la = i // 2
            w_out, wl = w_out_a, la
            regions = [dict(width=MIX_WIDTH, kind="gelu", f32="flat"),
                       dict(width=MIX_WIDTH, kind="gelu", f32="flat"),
                       dict(width=MEM_WIDTH, kind="plain", f32="flat")]
            wt = jnp.where(tril, spatial_w_a[la], 0.0).astype(BF16)
            bs = jnp.broadcast_to(spatial_b_a[la][:, :, None], (N_GROUPS, CHUNK, CHUNK))

            u, v, qm_s, w_in = _in_proj(xs, row2(norm_mix[i]), w_in_a, wl, regions, [], QK_DIM, tm=SAMPLE_ROWS,
                                        tn=512, emit_w=True)
            w0 = jnp.repeat(wt[:, 0, 0].astype(F32), CHUNK).reshape(1, MIX_WIDTH)
            b0 = jnp.repeat(spatial_b_a[la][:, 0], CHUNK).reshape(1, MIX_WIDTH)
            mix_s, vrow = _mixer_a_sample(u, v, row2(v_norm_a[la]), w0, b0)
            chunk_v_s.append(vrow[:n_seq_s])

            u, v, qm_p = _in_proj(xp, row2(norm_mix[i]), w_in, None, regions, [], QK_DIM, tm=1024, tn=512)
            mix_p = _mixer_a(u, v, row2(v_norm_a[la]), wt, bs, tm=512)
        else:
            lb = i // 2
            lam_init = 0.8 - 0.6 * math.exp(-0.3 * i)
            w_out, wl = w_out_b, lb
            lam_vecs = jnp.stack([lambda_q1[lb], lambda_k1[lb], lambda_q2[lb], lambda_k2[lb]])
            subln = row2(subln_b[lb])
            gains = [q_norm_b[lb], k_norm_b[lb]]
            qk_scale = QK_DIM ** -0.5

            regions = [dict(width=MIX_WIDTH, kind="gnorm", gain=0, f32="flat"),
                       dict(width=MIX_WIDTH, kind="gnorm", gain=1, f32="flat"),
                       dict(width=MIX_WIDTH, kind="plain", f32="flat"),
                       dict(width=MEM_WIDTH, kind="plain", f32="flat")]
            q, kf, vf, qm_s, w_in = _in_proj(xs, row2(norm_mix[i]), w_in_b, wl, regions, gains, QK_DIM,
                                             tm=SAMPLE_ROWS, tn=512, emit_w=True)
            k_s.append(kf[:n_seq_s])
            v_s.append(vf[:n_seq_s])
            qm = jnp.pad(head_view(q * qk_scale) * map_mask[None, None],
                         ((0, 0), (0, 0), (0, MAP_ROWS - 2), (0, 0))).astype(BF16)
            o = _diff_attn_sample(page_table, qm, ck, cv, lb, sample_bias, slope_col, head_view(kf), head_view(vf),
                                  lam_vecs, subln, lam_init)
            mix_s = jnp.pad(o.reshape(n_seq_s, MIX_WIDTH), ((0, SAMPLE_ROWS - n_seq_s), (0, 0))).astype(BF16)

            regions = [dict(width=MIX_WIDTH, kind="gnorm", gain=0, bf16="cols", scale=qk_scale),
                       dict(width=MIX_WIDTH, kind="gnorm", gain=1, f32="rows", bf16="rows"),
                       dict(width=MIX_WIDTH, kind="plain", f32="rows", bf16="cols"),
                       dict(width=MEM_WIDTH, kind="plain", f32="flat")]
            qt, kf, kb, vf, vt, qm_p = _in_proj(xp, row2(norm_mix[i]), w_in, None, regions, gains, QK_DIM, tm=1024,
                                                tn=512, seq=seq)
            k_p.append(kf)
            v_p.append(vf)
            mix_p = _diff_attn_prompt(qt, kb, vt, slopes, lam_vecs, subln, lam_init, n_seq_p, seq, t=1024)

        mo_s = _mem_attend_sample(qm_s, row2(mem_q_norm[i]), cmk, cmv, i, n_seq_s)
        xs, w_out_r = _out_proj_sample(mix_s, mo_s, w_out, wl, xs, tn=512)
        mo_p = _mem_attend(qm_p, row2(mem_q_norm[i]), mk, mv, rows_per_batch=seq, tm=512)
        xp = _out_proj(mix_p, mo_p, w_out_r, xp, tm=2048, tn=512)

        xs, a_s, w_gate, w_lin, w_down_r = _ffn_sample(
            xs, row2(norm_ffn[i]), w_up, conv_w[i], row2(conv_b[i]), conv_state[i, :, 0], conv_state[i, :, 1],
            w_down, i, tf=512)
        xp, cp = _ffn_prompt(xp, row2(norm_ffn[i]), w_gate, w_lin, conv_w[i], row2(conv_b[i]), w_down_r, seq,
                             tm=1024, tf=512)
        conv_p.append(cp)
        conv_s.append(jnp.stack([state_ffn_conv[i, :, 1], a_s[:n_seq_s]], axis=1))

    heads = lambda a, b, t: a.reshape(b, t, N_HEADS, HEAD_DIM)
    return (xp.reshape(n_seq_p, seq, D_MODEL),
            xs[:n_seq_s].reshape(n_seq_s, 1, D_MODEL),
            jnp.stack(mem_k_p).reshape(depth, n_seq_p, n_mem, MEM_HEADS, MEM_HEAD_DIM),
            jnp.stack(mem_v_p).reshape(depth, n_seq_p, n_mem, MEM_HEADS, MEM_HEAD_DIM),
            jnp.stack(k_p).transpose(0, 1, 3, 2, 4),
            jnp.stack(v_p).transpose(0, 1, 3, 2, 4),
            jnp.stack(conv_p),
            jnp.stack([heads(a, n_seq_s, 1) for a in k_s]),
            jnp.stack([heads(a, n_seq_s, 1) for a in v_s]),
            jnp.stack(conv_s),
            jnp.stack(chunk_v_s).reshape(len(chunk_v_s), n_seq_s, 1, MIX_WIDTH))
```
